```python
import math
import jax
import jax.numpy as jnp
from jax import lax
import numpy as np

D_MODEL = 2048
BATCH = 8
SEQ = 2048
DEPTH = 4

A_HEADS = 8
A_HEAD_DIM = 128
A_WIDTH = A_HEADS * A_HEAD_DIM
CHUNK = 128
B_HEADS = 8
B_HEAD_DIM = 128
B_WIDTH = B_HEADS * B_HEAD_DIM
Q_BLOCK = 128
IN_WIDTH = 2 * A_WIDTH + 3 * B_WIDTH + 2 * D_MODEL
N_EXPERTS = 16
N_GROUPS = 4
EXPERTS_PER_GROUP = N_EXPERTS // N_GROUPS
TOP_K = 2
GROUP_SCORE_K = 2
D_EXPERT = 1024
EXPERT_BLOCK = 128
PLE_DIM = 256
ALPHA = (2 * DEPTH) ** 0.25
DEEPNORM_BETA = (8 * DEPTH) ** -0.25
LN_EPS = 1e-5

kernel_name = "hybrid_gmlp_stickbreak_moe_deepnorm"


def _layer_norm(x, gain, bias):
    xf = x.astype(jnp.float32)
    mu = jnp.mean(xf, axis=-1, keepdims=True)
    xc = xf - mu
    var = jnp.mean(xc * xc, axis=-1, keepdims=True)
    y = xc * lax.rsqrt(var + LN_EPS) * gain.astype(jnp.float32) + bias.astype(jnp.float32)
    return y.astype(x.dtype)


def _chunked_spatial_gating(u, v, ws, bs):
    b, s, h, dh = v.shape
    nc = s // CHUNK
    vc = v.reshape(b, nc, CHUNK, h, dh)
    causal = jnp.tril(jnp.ones((CHUNK, CHUNK), dtype=ws.dtype))
    f = jnp.einsum('htp,bcphd->bcthd', ws * causal, vc)
    f = f + jnp.transpose(bs)[None, None, :, :, None]
    return u * f.reshape(b, s, h, dh)


def _stick_breaking_attention(q, k, v):
    b, s, h, dh = q.shape
    scale = dh ** -0.5
    outs = []
    for i in range(s // Q_BLOCK):
        kv_len = (i + 1) * Q_BLOCK
        qb = q[:, i * Q_BLOCK:kv_len]
        kb = k[:, :kv_len]
        vb = v[:, :kv_len]
        z = jnp.einsum('bqhd,bkhd->bhqk', qb, kb).astype(jnp.float32) * scale
        t_pos = i * Q_BLOCK + jnp.arange(Q_BLOCK)[:, None]
        s_pos = jnp.arange(kv_len)[None, :]
        causal = s_pos < t_pos
        log_1m_beta = jnp.where(causal, -jax.nn.softplus(z), 0.0)
        suffix = lax.cumsum(log_1m_beta, axis=3, reverse=True) - log_1m_beta
        log_a = jax.nn.log_sigmoid(z) + suffix
        a = jnp.where(causal, jnp.exp(log_a), 0.0)
        outs.append(jnp.einsum('bhqk,bkhd->bqhd', a.astype(vb.dtype), vb))
    return jnp.concatenate(outs, axis=1)


def _route(xf, router_w, router_bias):
    n = xf.shape[0]
    logits = (xf @ router_w).astype(jnp.float32)
    affinity = jax.nn.sigmoid(logits)
    sel = affinity + router_bias.astype(jnp.float32)
    sel_g = sel.reshape(n, N_GROUPS, EXPERTS_PER_GROUP)
    group_score = jnp.sum(lax.top_k(sel_g, GROUP_SCORE_K)[0], axis=-1)
    g_idx = jnp.argmax(group_score, axis=-1)
    within = jnp.take_along_axis(sel_g, g_idx[:, None, None], axis=1)[:, 0]
    _, local = lax.top_k(within, TOP_K)
    expert_idx = g_idx[:, None] * EXPERTS_PER_GROUP + local
    w = jnp.take_along_axis(affinity, expert_idx, axis=1)
    w = w / jnp.sum(w, axis=-1, keepdims=True)
    return expert_idx.astype(jnp.int32), w


def _moe(xf, expert_idx, gate_w, w_gate, w_up, w_down):
    n, d = xf.shape
    m = n * TOP_K
    flat_e = expert_idx.reshape(-1)
    flat_tok = jnp.repeat(jnp.arange(n, dtype=jnp.int32), TOP_K)
    flat_w = gate_w.reshape(-1)
    order = jnp.argsort(flat_e)
    se = flat_e[order]
    stok = flat_tok[order]
    sw = flat_w[order]
    counts = jnp.zeros((N_EXPERTS,), jnp.int32).at[flat_e].add(1)
    starts = jnp.cumsum(counts) - counts
    padded = (counts + EXPERT_BLOCK - 1) // EXPERT_BLOCK * EXPERT_BLOCK
    pends = jnp.cumsum(padded)
    pstarts = pends - padded
    dest = pstarts[se] + jnp.arange(m, dtype=jnp.int32) - starts[se]
    n_blocks = -(-m // EXPERT_BLOCK) + N_EXPERTS
    p_rows = n_blocks * EXPERT_BLOCK
    x_disp = jnp.zeros((p_rows, d), xf.dtype).at[dest].set(xf[stok])
    w_disp = jnp.zeros((p_rows,), sw.dtype).at[dest].set(sw)
    tok_disp = jnp.full((p_rows,), n, jnp.int32).at[dest].set(stok)
    block_start = jnp.arange(n_blocks, dtype=jnp.int32) * EXPERT_BLOCK
    block_e = jnp.minimum(jnp.searchsorted(pends, block_start, side='right'), N_EXPERTS - 1)

    def expert_block(args):
        xb, e = args
        hb = jax.nn.silu(xb @ w_gate[e]) * (xb @ w_up[e])
        return hb @ w_down[e]

    y = lax.map(expert_block, (x_disp.reshape(n_blocks, EXPERT_BLOCK, d), block_e))
    y = y.reshape(p_rows, d) * w_disp[:, None].astype(xf.dtype)
    return jax.ops.segment_sum(y, tok_disp, num_segments=n + 1)[:n]


def setup_inputs(seed: int = 0) -> dict:
    key = jax.random.key(seed)
    ks = jax.random.split(key, 24)

    def nrm(k, shape, scale):
        return jax.random.normal(k, shape, jnp.float32) * scale

    return {
        "x": nrm(ks[0], (BATCH, SEQ, D_MODEL), 1.0),
        "p": nrm(ks[1], (DEPTH, BATCH, SEQ, PLE_DIM), 1.0),
        "w_in": nrm(ks[2], (DEPTH, D_MODEL, IN_WIDTH), D_MODEL ** -0.5),
        "gmlp_ln_g": 1.0 + nrm(ks[3], (DEPTH, A_WIDTH), 0.02),
        "gmlp_ln_b": nrm(ks[4], (DEPTH, A_WIDTH), 0.02),
        "gmlp_ws": nrm(ks[5], (DEPTH, A_HEADS, CHUNK, CHUNK), CHUNK ** -0.5),
        "gmlp_bs": 1.0 + nrm(ks[6], (DEPTH, A_HEADS, CHUNK), 0.02),
        "w_out_a": nrm(ks[7], (DEPTH, A_WIDTH, D_MODEL), A_WIDTH ** -0.5),
        "w_out_b": nrm(ks[8], (DEPTH, B_WIDTH, D_MODEL), B_WIDTH ** -0.5),
        "w_o": nrm(ks[9], (DEPTH, D_MODEL, D_MODEL), D_MODEL ** -0.5 * DEEPNORM_BETA),
        "ln1_g": 1.0 + nrm(ks[10], (DEPTH, D_MODEL), 0.02),
        "ln1_b": nrm(ks[11], (DEPTH, D_MODEL), 0.02),
        "router_w": nrm(ks[12], (D_MODEL, N_EXPERTS), D_MODEL ** -0.5),
        "router_bias": nrm(ks[13], (N_EXPERTS,), 0.01),
        "exp_w_gate": nrm(ks[14], (DEPTH, N_EXPERTS, D_MODEL, D_EXPERT), D_MODEL ** -0.5),
        "exp_w_up": nrm(ks[15], (DEPTH, N_EXPERTS, D_MODEL, D_EXPERT), D_MODEL ** -0.5),
        "exp_w_down": nrm(ks[16], (DEPTH, N_EXPERTS, D_EXPERT, D_MODEL), D_EXPERT ** -0.5 * DEEPNORM_BETA),
        "ple_w_gate": nrm(ks[17], (DEPTH, D_MODEL, D_MODEL), D_MODEL ** -0.5),
        "ple_w_proj": nrm(ks[18], (DEPTH, PLE_DIM, D_MODEL), PLE_DIM ** -0.5 * DEEPNORM_BETA),
        "ln2_g": 1.0 + nrm(ks[19], (DEPTH, D_MODEL), 0.02),
        "ln2_b": nrm(ks[20], (DEPTH, D_MODEL), 0.02),
    }


def reference(x, p, w_in, gmlp_ln_g, gmlp_ln_b, gmlp_ws, gmlp_bs, w_out_a, w_out_b, w_o,
              ln1_g, ln1_b, router_w, router_bias, exp_w_gate, exp_w_up, exp_w_down,
              ple_w_gate, ple_w_proj, ln2_g, ln2_b):
    b, s, d = x.shape
    splits = [A_WIDTH, 2 * A_WIDTH, 2 * A_WIDTH + B_WIDTH, 2 * A_WIDTH + 2 * B_WIDTH,
              2 * A_WIDTH + 3 * B_WIDTH, 2 * A_WIDTH + 3 * B_WIDTH + D_MODEL]
    for i in range(DEPTH):
        h = x @ w_in[i]
        hu, hv, hq, hk, hvb, ga, gb = jnp.split(h, splits, axis=-1)
        u = jax.nn.gelu(hu)
        v = _layer_norm(jax.nn.gelu(hv), gmlp_ln_g[i], gmlp_ln_b[i])
        ya = _chunked_spatial_gating(u.reshape(b, s, A_HEADS, A_HEAD_DIM),
                                     v.reshape(b, s, A_HEADS, A_HEAD_DIM),
                                     gmlp_ws[i], gmlp_bs[i])
        ya = ya.reshape(b, s, A_WIDTH) @ w_out_a[i]
        ob = _stick_breaking_attention(hq.reshape(b, s, B_HEADS, B_HEAD_DIM),
                                       hk.reshape(b, s, B_HEADS, B_HEAD_DIM),
                                       hvb.reshape(b, s, B_HEADS, B_HEAD_DIM))
        yb = ob.reshape(b, s, B_WIDTH) @ w_out_b[i]
        mix = (jax.nn.sigmoid(ga) * ya + jax.nn.sigmoid(gb) * yb) @ w_o[i]
        x = _layer_norm(ALPHA * x + mix, ln1_g[i], ln1_b[i])
        xf = x.reshape(b * s, d)
        expert_idx, gate_w = _route(xf, router_w, router_bias)
        ffn = _moe(xf, expert_idx, gate_w, exp_w_gate[i], exp_w_up[i], exp_w_down[i]).reshape(b, s, d)
        ple = jax.nn.sigmoid(x @ ple_w_gate[i]) * (p[i] @ ple_w_proj[i])
        x = _layer_norm(ALPHA * x + ffn + ple, ln2_g[i], ln2_b[i])
    return x
```

```python
import functools

import jax
import jax.numpy as jnp
from jax import lax
from jax.experimental import pallas as pl
from jax.experimental.pallas import tpu as pltpu

D_MODEL = 2048
DEPTH = 4
A_HEADS = 8
A_HEAD_DIM = 128
A_WIDTH = A_HEADS * A_HEAD_DIM
CHUNK = 128
B_HEADS = 8
B_HEAD_DIM = 128
B_WIDTH = B_HEADS * B_HEAD_DIM
N_EXPERTS = 16
N_GROUPS = 4
EXPERTS_PER_GROUP = N_EXPERTS // N_GROUPS
TOP_K = 2
D_EXPERT = 1024
PLE_DIM = 256
ALPHA = (2 * DEPTH) ** 0.25
LN_EPS = 1e-5

LANES = 128
VMEM_LIMIT = 56 * 1024 * 1024

F32 = jnp.float32
BF16 = jnp.bfloat16

INPROJ_TM = 1024
INPROJ_TN = 1024
GMLP_T = 512
ATTN_T = 256
MERGE_T = 256
DISPATCH_T = 256
EXPERT_TM = 256
FINAL_T = 256


def _layer_norm(xf, gain, bias):
    mu = jnp.mean(xf, axis=-1, keepdims=True)
    xc = xf - mu
    var = jnp.mean(xc * xc, axis=-1, keepdims=True)
    return xc * lax.rsqrt(var + LN_EPS) * gain + bias


def _const_spec(shape):
    nd = len(shape)
    return pl.BlockSpec(shape, lambda *_: (0,) * nd, pipeline_mode=pl.Buffered(1))


def _inproj_kernel(x_ref, w_ref, o_ref, *, act):
    acc = jnp.dot(x_ref[...], w_ref[...], preferred_element_type=F32)
    o_ref[...] = act(acc).astype(o_ref.dtype)


def _inproj(xb, w, col0, ncols, act):
    m, k = xb.shape
    tm, tn = INPROJ_TM, INPROJ_TN
    cb0 = col0 // tn
    return pl.pallas_call(
        functools.partial(_inproj_kernel, act=act),
        grid=(m // tm, ncols // tn),
        in_specs=[
            pl.BlockSpec((tm, k), lambda i, j: (i, 0)),
            pl.BlockSpec((k, tn), lambda i, j: (0, cb0 + j)),
        ],
        out_specs=pl.BlockSpec((tm, tn), lambda i, j: (i, j)),
        out_shape=jax.ShapeDtypeStruct((m, ncols), BF16),
        compiler_params=pltpu.CompilerParams(
            dimension_semantics=("parallel", "parallel"), vmem_limit_bytes=VMEM_LIMIT),
        name="inproj",
    )(xb, w)


def _gmlp_kernel(u_ref, gv_ref, lng_ref, lnb_ref, ws_ref, bias_ref, wo_ref, o_ref, yin_ref):
    t = u_ref.shape[0]
    nc = t // CHUNK
    v = _layer_norm(gv_ref[...].astype(F32), lng_ref[...], lnb_ref[...]).astype(BF16)
    row = lax.broadcasted_iota(jnp.int32, (CHUNK, CHUNK), 0)
    col = lax.broadcasted_iota(jnp.int32, (CHUNK, CHUNK), 1)
    tril = row >= col
    for h in range(A_HEADS):
        hs = slice(h * A_HEAD_DIM, (h + 1) * A_HEAD_DIM)
        wsm = jnp.where(tril, ws_ref[h], 0.0).astype(BF16)
        vh = jnp.concatenate([v[c * CHUNK:(c + 1) * CHUNK, hs] for c in range(nc)], axis=1)
        f = jnp.dot(wsm, vh, preferred_element_type=F32)
        bh = bias_ref[:, hs]
        for c in range(nc):
            rs = slice(c * CHUNK, (c + 1) * CHUNK)
            fc = f[:, c * A_HEAD_DIM:(c + 1) * A_HEAD_DIM] + bh
            yin_ref[rs, hs] = (u_ref[rs, hs].astype(F32) * fc).astype(BF16)
    o_ref[...] = jnp.dot(yin_ref[...], wo_ref[...], preferred_element_type=F32).astype(o_ref.dtype)


def _gmlp(ug, lng, lnb, ws, bias, wo):
    m = ug.shape[0]
    t = GMLP_T
    return pl.pallas_call(
        _gmlp_kernel,
        grid=(m // t,),
        in_specs=[
            pl.BlockSpec((t, A_WIDTH), lambda i: (i, 0)),
            pl.BlockSpec((t, A_WIDTH), lambda i: (i, 1)),
            _const_spec((1, A_WIDTH)),
            _const_spec((1, A_WIDTH)),
            _const_spec((A_HEADS, CHUNK, CHUNK)),
            _const_spec((CHUNK, A_WIDTH)),
            _const_spec((A_WIDTH, D_MODEL)),
        ],
        out_specs=pl.BlockSpec((t, D_MODEL), lambda i: (i, 0)),
        out_shape=jax.ShapeDtypeStruct((m, D_MODEL), BF16),
        scratch_shapes=[pltpu.VMEM((t, A_WIDTH), BF16)],
        compiler_params=pltpu.CompilerParams(
            dimension_semantics=("parallel",), vmem_limit_bytes=VMEM_LIMIT),
        name="gmlp",
    )(ug, ug, lng, lnb, ws, bias, wo)


def _attn_kernel(q_ref, k_ref, v_ref, o_ref):
    t = q_ref.shape[0]
    i = pl.program_id(2)
    q = q_ref[...]
    scale = B_HEAD_DIM ** -0.5
    row = lax.broadcasted_iota(jnp.int32, (t, t), 0)
    col = lax.broadcasted_iota(jnp.int32, (t, t), 1)
    upper = jnp.where(row > col, 1.0, 0.0).astype(BF16)
    causal = col < row

    def tile(kj, vj, carry, acc, diag):
        z = lax.dot_general(q, kj, (((1,), (1,)), ((), ())), preferred_element_type=F32) * scale
        l = jnp.log1p(jnp.exp(-jnp.abs(z)))
        lm = -(jnp.maximum(z, 0.0) + l)
        logsig = jnp.minimum(z, 0.0) - l
        if diag:
            lm = jnp.where(causal, lm, 0.0)
        lm_hi = lm.astype(BF16)
        lm_lo = (lm - lm_hi.astype(F32)).astype(BF16)
        suffix = (jnp.dot(lm_hi, upper, preferred_element_type=F32)
                  + jnp.dot(lm_lo, upper, preferred_element_type=F32))
        a = jnp.exp(logsig + suffix + carry)
        if diag:
            a = jnp.where(causal, a, 0.0)
        acc = acc + jnp.dot(a.astype(BF16), vj, preferred_element_type=F32)
        carry = carry + jnp.sum(lm, axis=1, keepdims=True)
        return carry, acc

    off = pl.multiple_of(i * t, t)
    carry, acc = tile(k_ref[pl.ds(off, t), :], v_ref[pl.ds(off, t), :],
                      jnp.zeros((t, 1), F32), jnp.zeros((t, B_HEAD_DIM), F32), True)

    def body(jj, c):
        o = pl.multiple_of((i - 1 - jj) * t, t)
        return tile(k_ref[pl.ds(o, t), :], v_ref[pl.ds(o, t), :], c[0], c[1], False)

    carry, acc = lax.fori_loop(0, i, body, (carry, acc))
    o_ref[...] = acc.astype(o_ref.dtype)


def _attention(qkv, batch, seq):
    m = qkv.shape[0]
    t = ATTN_T
    nq = seq // t
    hb = B_WIDTH // B_HEAD_DIM
    return pl.pallas_call(
        _attn_kernel,
        grid=(batch, B_HEADS, nq),
        in_specs=[
            pl.BlockSpec((t, B_HEAD_DIM), lambda b, h, i: (b * nq + i, h)),
            pl.BlockSpec((seq, B_HEAD_DIM), lambda b, h, i: (b, hb + h)),
            pl.BlockSpec((seq, B_HEAD_DIM), lambda b, h, i: (b, 2 * hb + h)),
        ],
        out_specs=pl.BlockSpec((t, B_HEAD_DIM), lambda b, h, i: (b * nq + i, h)),
        out_shape=jax.ShapeDtypeStruct((m, B_WIDTH), BF16),
        compiler_params=pltpu.CompilerParams(
            dimension_semantics=("parallel", "parallel", "parallel"), vmem_limit_bytes=VMEM_LIMIT),
        name="attn",
    )(qkv, qkv, qkv)


def _first_max(vals):
    best, idx = vals[0], jnp.zeros(vals[0].shape, jnp.int32)
    for j in range(1, len(vals)):
        better = vals[j] > best
        best = jnp.where(better, vals[j], best)
        idx = jnp.where(better, j, idx)
    return best, idx


def _pick(idx, vals):
    out = vals[0]
    for j in range(1, len(vals)):
        out = jnp.where(idx == j, vals[j], out)
    return out


def _merge_kernel(ob_ref, ya_ref, sga_ref, sgb_ref, x_ref, wob_ref, wo_ref, g_ref, b_ref,
                  rw_ref, rb_ref, x1_ref, eidx_ref, rank_ref, gwt_ref, cnt_ref, base_ref):
    t = x_ref.shape[0]

    @pl.when(pl.program_id(0) == 0)
    def _():
        base_ref[...] = jnp.zeros_like(base_ref)

    yb = jnp.dot(ob_ref[...], wob_ref[...], preferred_element_type=F32)
    mixin = sga_ref[...].astype(F32) * ya_ref[...].astype(F32) + sgb_ref[...].astype(F32) * yb
    mix = jnp.dot(mixin.astype(BF16), wo_ref[...], preferred_element_type=F32)
    x1 = _layer_norm(ALPHA * x_ref[...] + mix, g_ref[...], b_ref[...])
    x1_ref[...] = x1

    logits = jnp.dot(x1, rw_ref[...], preferred_element_type=F32,
                     precision=lax.Precision.HIGHEST)
    lt = jnp.transpose(logits)[:N_EXPERTS, :]
    aff = jax.nn.sigmoid(lt)
    sel = aff + rb_ref[...]
    sel_rows = [sel[e:e + 1, :] for e in range(N_EXPERTS)]
    aff_rows = [aff[e:e + 1, :] for e in range(N_EXPERTS)]
    gscore = []
    for g in range(N_GROUPS):
        s0, s1, s2, s3 = sel_rows[4 * g:4 * g + 4]
        hi1, lo1 = jnp.maximum(s0, s1), jnp.minimum(s0, s1)
        hi2, lo2 = jnp.maximum(s2, s3), jnp.minimum(s2, s3)
        top1 = jnp.maximum(hi1, hi2)
        top2 = jnp.maximum(jnp.minimum(hi1, hi2), jnp.maximum(lo1, lo2))
        gscore.append(top1 + top2)
    _, gidx = _first_max(gscore)
    within = [_pick(gidx, [sel_rows[4 * g + j] for g in range(N_GROUPS)])
              for j in range(EXPERTS_PER_GROUP)]
    awithin = [_pick(gidx, [aff_rows[4 * g + j] for g in range(N_GROUPS)])
               for j in range(EXPERTS_PER_GROUP)]
    _, i0 = _first_max(within)
    masked = [jnp.where(i0 == j, -jnp.inf, within[j]) for j in range(EXPERTS_PER_GROUP)]
    _, i1 = _first_max(masked)
    a0, a1 = _pick(i0, awithin), _pick(i1, awithin)
    denom = a0 + a1
    e0 = gidx * EXPERTS_PER_GROUP + i0
    e1 = gidx * EXPERTS_PER_GROUP + i1
    slot = lax.broadcasted_iota(jnp.int32, (TOP_K, t), 0)
    eidx_ref[...] = jnp.where(slot == 0, e0, e1)

    eiota = lax.broadcasted_iota(jnp.int32, (N_EXPERTS, t), 0)
    oh0 = jnp.where(eiota == e0, 1.0, 0.0)
    oh1 = jnp.where(eiota == e1, 1.0, 0.0)
    oh = oh0 + oh1
    row = lax.broadcasted_iota(jnp.int32, (t, t), 0)
    col = lax.broadcasted_iota(jnp.int32, (t, t), 1)
    before = jnp.where(row < col, 1.0, 0.0).astype(BF16)
    tot = base_ref[:, 0:1] + jnp.dot(oh.astype(BF16), before, preferred_element_type=F32)
    r0 = jnp.sum(oh0 * tot, axis=0, keepdims=True)
    r1 = jnp.sum(oh1 * tot, axis=0, keepdims=True)
    rank_ref[...] = jnp.where(slot == 0, r0, r1).astype(jnp.int32)
    newbase = base_ref[...] + jnp.sum(oh, axis=1, keepdims=True)
    base_ref[...] = newbase
    cnt_ref[...] = newbase

    wrow = lax.broadcasted_iota(jnp.int32, (LANES, t), 0)
    gw = jnp.where(wrow == 0, a0 / denom, jnp.where(wrow == 1, a1 / denom, 0.0))
    gwt_ref[...] = jnp.transpose(gw)


def _merge(ob, ya, gates, xf, wob, wo, g, b, rw, rb):
    m = xf.shape[0]
    t = MERGE_T
    return pl.pallas_call(
        _merge_kernel,
        grid=(m // t,),
        in_specs=[
            pl.BlockSpec((t, B_WIDTH), lambda i: (i, 0)),
            pl.BlockSpec((t, D_MODEL), lambda i: (i, 0)),
            pl.BlockSpec((t, D_MODEL), lambda i: (i, 0)),
            pl.BlockSpec((t, D_MODEL), lambda i: (i, 1)),
            pl.BlockSpec((t, D_MODEL), lambda i: (i, 0)),
            _const_spec((B_WIDTH, D_MODEL)),
            _const_spec((D_MODEL, D_MODEL)),
            _const_spec((1, D_MODEL)),
            _const_spec((1, D_MODEL)),
            _const_spec((D_MODEL, LANES)),
            _const_spec((N_EXPERTS, 1)),
        ],
        out_specs=[
            pl.BlockSpec((t, D_MODEL), lambda i: (i, 0)),
            pl.BlockSpec((TOP_K, t), lambda i: (0, i)),
            pl.BlockSpec((TOP_K, t), lambda i: (0, i)),
            pl.BlockSpec((t, LANES), lambda i: (i, 0)),
            pl.BlockSpec((N_EXPERTS, LANES), lambda i: (0, 0)),
        ],
        out_shape=[
            jax.ShapeDtypeStruct((m, D_MODEL), F32),
            jax.ShapeDtypeStruct((TOP_K, m), jnp.int32),
            jax.ShapeDtypeStruct((TOP_K, m), jnp.int32),
            jax.ShapeDtypeStruct((m, LANES), F32),
            jax.ShapeDtypeStruct((N_EXPERTS, LANES), F32),
        ],
        scratch_shapes=[pltpu.VMEM((N_EXPERTS, LANES), F32)],
        compiler_params=pltpu.CompilerParams(
            dimension_semantics=("arbitrary",), vmem_limit_bytes=VMEM_LIMIT),
        name="merge",
    )(ob, ya, gates, gates, xf, wob, wo, g, b, rw, rb)


def _dispatch_row_copy(x_ref, xd_ref, sem, src_row, dst_row):
    return pltpu.make_async_copy(x_ref.at[pl.ds(src_row, 1), :], xd_ref.at[pl.ds(dst_row, 1), :], sem)


def _dispatch_kernel(dest_ref, x_ref, xd_in_ref, xd_ref, sem):
    del xd_in_ref
    t = x_ref.shape[0]
    m = dest_ref.shape[0] // TOP_K
    base = pl.program_id(0) * t

    def issue(r, c):
        for k in range(TOP_K):
            _dispatch_row_copy(x_ref, xd_ref, sem, r, dest_ref[k * m + base + r]).start()
        return c

    lax.fori_loop(0, t, issue, 0)

    def drain(r, c):
        for k in range(TOP_K):
            _dispatch_row_copy(x_ref, xd_ref, sem, r, dest_ref[k * m + base + r]).wait()
        return c

    lax.fori_loop(0, t, drain, 0)


def _dispatch(dest, x1, p_rows):
    m, d = x1.shape
    t = DISPATCH_T
    zeros = jnp.zeros((p_rows, d), x1.dtype)
    return pl.pallas_call(
        _dispatch_kernel,
        grid_spec=pltpu.PrefetchScalarGridSpec(
            num_scalar_prefetch=1,
            grid=(m // t,),
            in_specs=[
                pl.BlockSpec((t, d), lambda i, dest: (i, 0)),
                pl.BlockSpec(memory_space=pl.ANY),
            ],
            out_specs=pl.BlockSpec(memory_space=pl.ANY),
            scratch_shapes=[pltpu.SemaphoreType.DMA(())],
        ),
        out_shape=jax.ShapeDtypeStruct((p_rows, d), x1.dtype),
        input_output_aliases={2: 0},
        compiler_params=pltpu.CompilerParams(
            dimension_semantics=("arbitrary",), vmem_limit_bytes=VMEM_LIMIT),
        name="dispatch",
    )(dest, x1, zeros)


def _expert_kernel(be_ref, nv_ref, x_ref, wg_ref, wu_ref, wd_ref, y_ref):
    del be_ref
    b = pl.program_id(0)

    @pl.when(b < nv_ref[0])
    def _():
        xb = x_ref[...].astype(BF16)
        g = jnp.dot(xb, wg_ref[...], preferred_element_type=F32)
        u = jnp.dot(xb, wu_ref[...], preferred_element_type=F32)
        h = (jax.nn.silu(g) * u).astype(BF16)
        y_ref[...] = jnp.dot(h, wd_ref[...], preferred_element_type=F32).astype(y_ref.dtype)

    @pl.when(b >= nv_ref[0])
    def _():
        y_ref[...] = jnp.zeros_like(y_ref)


def _experts(block_e, n_valid, xd, wg, wu, wd):
    p_rows, d = xd.shape
    tm = EXPERT_TM

    def x_map(b, be, nv):
        return (jnp.minimum(b, nv[0] - 1), 0)

    return pl.pallas_call(
        _expert_kernel,
        grid_spec=pltpu.PrefetchScalarGridSpec(
            num_scalar_prefetch=2,
            grid=(p_rows // tm,),
            in_specs=[
                pl.BlockSpec((tm, d), x_map),
                pl.BlockSpec((None, d, D_EXPERT), lambda b, be, nv: (be[b], 0, 0)),
                pl.BlockSpec((None, d, D_EXPERT), lambda b, be, nv: (be[b], 0, 0)),
                pl.BlockSpec((None, D_EXPERT, d), lambda b, be, nv: (be[b], 0, 0)),
            ],
            out_specs=pl.BlockSpec((tm, d), lambda b, be, nv: (b, 0)),
        ),
        out_shape=jax.ShapeDtypeStruct((p_rows, d), F32),
        compiler_params=pltpu.CompilerParams(
            dimension_semantics=("arbitrary",), vmem_limit_bytes=VMEM_LIMIT),
        name="experts",
    )(block_e, n_valid, xd, wg, wu, wd)


def _combine_row_copy(y_ref, buf_ref, sem, k, r, src_row):
    return pltpu.make_async_copy(y_ref.at[pl.ds(src_row, 1), :], buf_ref.at[k, pl.ds(r, 1), :], sem)


def _final_kernel(dest_ref, x1_ref, p_ref, gwt_ref, wpg_ref, wpp_ref, g_ref, b_ref, y_ref,
                  xo_ref, xob_ref, buf_ref, sem):
    t = x1_ref.shape[0]
    m = dest_ref.shape[0] // TOP_K
    base = pl.program_id(0) * t

    def issue(r, c):
        for k in range(TOP_K):
            _combine_row_copy(y_ref, buf_ref, sem, k, r, dest_ref[k * m + base + r]).start()
        return c

    lax.fori_loop(0, t, issue, 0)

    x1 = x1_ref[...]
    gate = jax.nn.sigmoid(jnp.dot(x1.astype(BF16), wpg_ref[...], preferred_element_type=F32))
    proj = jnp.dot(p_ref[...].astype(BF16), wpp_ref[...], preferred_element_type=F32)
    acc = ALPHA * x1 + gate * proj

    def drain(r, c):
        for k in range(TOP_K):
            _combine_row_copy(y_ref, buf_ref, sem, k, r, dest_ref[k * m + base + r]).wait()
        return c

    lax.fori_loop(0, t, drain, 0)

    gw = gwt_ref[...]
    ffn = gw[:, 0:1] * buf_ref[0] + gw[:, 1:2] * buf_ref[1]
    x2 = _layer_norm(acc + ffn, g_ref[...], b_ref[...])
    xo_ref[...] = x2
    xob_ref[...] = x2.astype(BF16)


def _final(dest, x1, p_l, gwt, wpg, wpp, g, b, y):
    m, d = x1.shape
    t = FINAL_T
    return pl.pallas_call(
        _final_kernel,
        grid_spec=pltpu.PrefetchScalarGridSpec(
            num_scalar_prefetch=1,
            grid=(m // t,),
            in_specs=[
                pl.BlockSpec((t, d), lambda i, dest: (i, 0)),
                pl.BlockSpec((t, PLE_DIM), lambda i, dest: (i, 0)),
                pl.BlockSpec((t, LANES), lambda i, dest: (i, 0)),
                pl.BlockSpec((d, d), lambda i, dest: (0, 0), pipeline_mode=pl.Buffered(1)),
                pl.BlockSpec((PLE_DIM, d), lambda i, dest: (0, 0), pipeline_mode=pl.Buffered(1)),
                pl.BlockSpec((1, d), lambda i, dest: (0, 0), pipeline_mode=pl.Buffered(1)),
                pl.BlockSpec((1, d), lambda i, dest: (0, 0), pipeline_mode=pl.Buffered(1)),
                pl.BlockSpec(memory_space=pl.ANY),
            ],
            out_specs=[
                pl.BlockSpec((t, d), lambda i, dest: (i, 0)),
                pl.BlockSpec((t, d), lambda i, dest: (i, 0)),
            ],
            scratch_shapes=[pltpu.VMEM((TOP_K, t, d), F32), pltpu.SemaphoreType.DMA(())],
        ),
        out_shape=[jax.ShapeDtypeStruct((m, d), F32), jax.ShapeDtypeStruct((m, d), BF16)],
        compiler_params=pltpu.CompilerParams(
            dimension_semantics=("arbitrary",), vmem_limit_bytes=VMEM_LIMIT),
        name="final",
    )(dest, x1, p_l, gwt, wpg, wpp, g, b, y)


def kernel(x, p, w_in, gmlp_ln_g, gmlp_ln_b, gmlp_ws, gmlp_bs, w_out_a, w_out_b, w_o, ln1_g, ln1_b,
           router_w, router_bias, exp_w_gate, exp_w_up, exp_w_down, ple_w_gate, ple_w_proj,
           ln2_g, ln2_b):
    batch, seq, d = x.shape
    m = batch * seq
    tm = EXPERT_TM
    n_blocks = (m * TOP_K) // tm + N_EXPERTS
    p_rows = n_blocks * tm

    xf = x.reshape(m, d)
    xb = xf.astype(BF16)
    p_flat = p.reshape(DEPTH, m, PLE_DIM)
    rw = jnp.pad(router_w.astype(F32), ((0, 0), (0, LANES - N_EXPERTS)))
    rb = router_bias.astype(F32).reshape(N_EXPERTS, 1)

    for i in range(DEPTH):
        w_in_b = w_in[i].astype(BF16)
        ug = _inproj(xb, w_in_b, 0, 2 * A_WIDTH, jax.nn.gelu)
        qkv = _inproj(xb, w_in_b, 2 * A_WIDTH, 3 * B_WIDTH, lambda a: a)
        gates = _inproj(xb, w_in_b, 2 * A_WIDTH + 3 * B_WIDTH, 2 * D_MODEL, jax.nn.sigmoid)

        bias = jnp.repeat(jnp.transpose(gmlp_bs[i]), A_HEAD_DIM, axis=1)
        ya = _gmlp(ug, gmlp_ln_g[i].reshape(1, A_WIDTH), gmlp_ln_b[i].reshape(1, A_WIDTH),
                   gmlp_ws[i], bias, w_out_a[i].astype(BF16))
        ob = _attention(qkv, batch, seq)

        x1, eidx, rank, gwt, cnt = _merge(
            ob, ya, gates, xf, w_out_b[i].astype(BF16), w_o[i].astype(BF16),
            ln1_g[i].reshape(1, d), ln1_b[i].reshape(1, d), rw, rb)

        counts = cnt[:, 0].astype(jnp.int32)
        padded = (counts + tm - 1) // tm * tm
        pends = jnp.cumsum(padded)
        pstarts = pends - padded
        dest = (pstarts[eidx] + rank).reshape(TOP_K * m)
        block_start = jnp.arange(n_blocks, dtype=jnp.int32) * tm
        block_e = jnp.minimum(jnp.searchsorted(pends, block_start, side="right"),
                              N_EXPERTS - 1).astype(jnp.int32)
        n_valid = (pends[-1:] // tm).astype(jnp.int32)

        xd = _dispatch(dest, x1, p_rows)
        y = _experts(block_e, n_valid, xd, exp_w_gate[i].astype(BF16), exp_w_up[i].astype(BF16),
                     exp_w_down[i].astype(BF16))
        xf, xb = _final(dest, x1, p_flat[i], gwt, ple_w_gate[i].astype(BF16),
                        ple_w_proj[i].astype(BF16), ln2_g[i].reshape(1, d), ln2_b[i].reshape(1, d), y)

    return xf.reshape(batch, seq, d)
```

```python
import functools

import jax
import jax.numpy as jnp
from jax import lax
from jax.experimental import pallas as pl
from jax.experimental.pallas import tpu as pltpu

D_MODEL = 2048
DEPTH = 4
A_HEADS = 8
A_HEAD_DIM = 128
A_WIDTH = A_HEADS * A_HEAD_DIM
CHUNK = 128
B_HEADS = 8
B_HEAD_DIM = 128
B_WIDTH = B_HEADS * B_HEAD_DIM
N_EXPERTS = 16
N_GROUPS = 4
EXPERTS_PER_GROUP = N_EXPERTS // N_GROUPS
TOP_K = 2
D_EXPERT = 1024
PLE_DIM = 256
ALPHA = (2 * DEPTH) ** 0.25
LN_EPS = 1e-5
LOG2E = 1.4426950408889634

LANES = 128
VMEM_LIMIT = 56 * 1024 * 1024

F32 = jnp.float32
BF16 = jnp.bfloat16

INPROJ_TM = 1024
INPROJ_TN = 1024
GMLP_T = 512
ATTN_T = 256
ATTN_HEADS_PER_STEP = 4
MERGE_T = 256
DISPATCH_T = 256
EXPERT_TM = 256
FINAL_T = 256


def _layer_norm(xf, gain, bias):
    mu = jnp.mean(xf, axis=-1, keepdims=True)
    xc = xf - mu
    var = jnp.mean(xc * xc, axis=-1, keepdims=True)
    return xc * lax.rsqrt(var + LN_EPS) * gain + bias


def _const_spec(shape):
    nd = len(shape)
    return pl.BlockSpec(shape, lambda *_: (0,) * nd, pipeline_mode=pl.Buffered(1))


def _inproj_kernel(x_ref, w_ref, o_ref, wb_ref, *, act, first_tile_scale):
    @pl.when(pl.program_id(1) == 0)
    def _():
        wb_ref[...] = w_ref[...].astype(BF16)

    y = act(jnp.dot(x_ref[...], wb_ref[...], preferred_element_type=F32))
    if first_tile_scale is not None:
        y = y * jnp.where(pl.program_id(0) == 0, first_tile_scale, 1.0)
    o_ref[...] = y.astype(o_ref.dtype)


def _inproj(xb, w, col0, ncols, act, first_tile_scale=None):
    m, k = xb.shape
    tm, tn = INPROJ_TM, INPROJ_TN
    cb0 = col0 // tn
    return pl.pallas_call(
        functools.partial(_inproj_kernel, act=act, first_tile_scale=first_tile_scale),
        grid=(ncols // tn, m // tm),
        in_specs=[
            pl.BlockSpec((tm, k), lambda j, i: (i, 0)),
            pl.BlockSpec((k, tn), lambda j, i: (0, cb0 + j), pipeline_mode=pl.Buffered(1)),
        ],
        out_specs=pl.BlockSpec((tm, tn), lambda j, i: (i, j)),
        out_shape=jax.ShapeDtypeStruct((m, ncols), BF16),
        scratch_shapes=[pltpu.VMEM((k, tn), BF16)],
        compiler_params=pltpu.CompilerParams(
            dimension_semantics=("arbitrary", "arbitrary"), vmem_limit_bytes=VMEM_LIMIT),
        name="inproj",
    )(xb, w)


def _gmlp_kernel(u_ref, gv_ref, lng_ref, lnb_ref, ws_ref, bias_ref, wo_ref, o_ref, yin_ref):
    t = u_ref.shape[0]
    nc = t // CHUNK
    v = _layer_norm(gv_ref[...].astype(F32), lng_ref[...], lnb_ref[...]).astype(BF16)
    row = lax.broadcasted_iota(jnp.int32, (CHUNK, CHUNK), 0)
    col = lax.broadcasted_iota(jnp.int32, (CHUNK, CHUNK), 1)
    tril = row >= col
    for h in range(A_HEADS):
        hs = slice(h * A_HEAD_DIM, (h + 1) * A_HEAD_DIM)
        wsm = jnp.where(tril, ws_ref[h], 0.0).astype(BF16)
        vh = jnp.concatenate([v[c * CHUNK:(c + 1) * CHUNK, hs] for c in range(nc)], axis=1)
        f = jnp.dot(wsm, vh, preferred_element_type=F32)
        bh = bias_ref[:, hs]
        for c in range(nc):
            rs = slice(c * CHUNK, (c + 1) * CHUNK)
            fc = f[:, c * A_HEAD_DIM:(c + 1) * A_HEAD_DIM] + bh
            yin_ref[rs, hs] = (u_ref[rs, hs].astype(F32) * fc).astype(BF16)
    o_ref[...] = jnp.dot(yin_ref[...], wo_ref[...], preferred_element_type=F32).astype(o_ref.dtype)


def _gmlp(ug, lng, lnb, ws, bias, wo):
    m = ug.shape[0]
    t = GMLP_T
    return pl.pallas_call(
        _gmlp_kernel,
        grid=(m // t,),
        in_specs=[
            pl.BlockSpec((t, A_WIDTH), lambda i: (i, 0)),
            pl.BlockSpec((t, A_WIDTH), lambda i: (i, 1)),
            _const_spec((1, A_WIDTH)),
            _const_spec((1, A_WIDTH)),
            _const_spec((A_HEADS, CHUNK, CHUNK)),
            _const_spec((CHUNK, A_WIDTH)),
            _const_spec((A_WIDTH, D_MODEL)),
        ],
        out_specs=pl.BlockSpec((t, D_MODEL), lambda i: (i, 0)),
        out_shape=jax.ShapeDtypeStruct((m, D_MODEL), BF16),
        scratch_shapes=[pltpu.VMEM((t, A_WIDTH), BF16)],
        compiler_params=pltpu.CompilerParams(
            dimension_semantics=("parallel",), vmem_limit_bytes=VMEM_LIMIT),
        name="gmlp",
    )(ug, ug, lng, lnb, ws, bias, wo)


def _attn_kernel(q_ref, k_ref, v_ref, o_ref):
    t = q_ref.shape[0]
    nh = q_ref.shape[1] // B_HEAD_DIM
    heads = range(nh)
    i = pl.program_id(2)
    row = lax.broadcasted_iota(jnp.int32, (t, t), 0)
    col = lax.broadcasted_iota(jnp.int32, (t, t), 1)
    neg_upper = jnp.where(row > col, -1.0, 0.0).astype(BF16)
    causal = col < row
    hs = [slice(h * B_HEAD_DIM, (h + 1) * B_HEAD_DIM) for h in heads]
    qs = [q_ref[:, s] for s in hs]

    def tiles(off, state, mask):
        z = [lax.dot_general(qs[h], k_ref[pl.ds(off, t), hs[h]], (((1,), (1,)), ((), ())),
                             preferred_element_type=F32) for h in heads]
        sp, logsig, packed = [], [], []
        for h in heads:
            neg_abs = pltpu.bitcast(pltpu.bitcast(z[h], jnp.uint32) | jnp.uint32(0x80000000), F32)
            s = jnp.maximum(z[h], 0.0) + jnp.log(1.0 + jnp.exp2(neg_abs)) * LOG2E
            logsig.append(z[h] - s)
            if mask is not None:
                s = jnp.where(mask, s, 0.0)
            sp.append(s)
            packed.append(s.astype(BF16))
        suffix = [jnp.dot(packed[h], neg_upper, preferred_element_type=F32) for h in heads]
        a = []
        for h in heads:
            ah = jnp.exp2(logsig[h] + suffix[h] + state[h][0])
            if mask is not None:
                ah = jnp.where(mask, ah, 0.0)
            a.append(ah.astype(BF16))
        return tuple(
            (state[h][0] - jnp.sum(sp[h], axis=1, keepdims=True),
             state[h][1] + jnp.dot(a[h], v_ref[pl.ds(off, t), hs[h]], preferred_element_type=F32))
            for h in heads)

    init = tuple((jnp.zeros((t, 1), F32), jnp.zeros((t, B_HEAD_DIM), F32)) for _ in range(nh))
    state = tiles(pl.multiple_of(i * t, t), init, causal)

    def body(jj, st):
        return tiles(pl.multiple_of((i - 1 - jj) * t, t), st, None)

    state = lax.fori_loop(0, i, body, state)
    for h in range(nh):
        o_ref[:, hs[h]] = state[h][1].astype(o_ref.dtype)


def _attention(qkv, batch, seq):
    m = qkv.shape[0]
    t = ATTN_T
    nq = seq // t
    w = ATTN_HEADS_PER_STEP * B_HEAD_DIM
    hb = B_WIDTH // w
    return pl.pallas_call(
        _attn_kernel,
        grid=(batch, hb, nq),
        in_specs=[
            pl.BlockSpec((t, w), lambda b, h, i: (b * nq + i, h)),
            pl.BlockSpec((seq, w), lambda b, h, i: (b, hb + h)),
            pl.BlockSpec((seq, w), lambda b, h, i: (b, 2 * hb + h)),
        ],
        out_specs=pl.BlockSpec((t, w), lambda b, h, i: (b * nq + i, h)),
        out_shape=jax.ShapeDtypeStruct((m, B_WIDTH), BF16),
        compiler_params=pltpu.CompilerParams(
            dimension_semantics=("parallel", "parallel", "parallel"), vmem_limit_bytes=VMEM_LIMIT),
        name="attn",
    )(qkv, qkv, qkv)


def _first_max(vals):
    best, idx = vals[0], jnp.zeros(vals[0].shape, jnp.int32)
    for j in range(1, len(vals)):
        better = vals[j] > best
        best = jnp.where(better, vals[j], best)
        idx = jnp.where(better, j, idx)
    return best, idx


def _pick(idx, vals):
    out = vals[0]
    for j in range(1, len(vals)):
        out = jnp.where(idx == j, vals[j], out)
    return out


def _merge_kernel(ob_ref, ya_ref, sga_ref, sgb_ref, x_ref, wob_ref, wo_ref, g_ref, b_ref,
                  rw_ref, rb_ref, x1_ref, eidx_ref, rank_ref, gwt_ref, cnt_ref, base_ref):
    t = x_ref.shape[0]

    @pl.when(pl.program_id(0) == 0)
    def _():
        base_ref[...] = jnp.zeros_like(base_ref)

    yb = jnp.dot(ob_ref[...], wob_ref[...], preferred_element_type=F32)
    mixin = sga_ref[...].astype(F32) * ya_ref[...].astype(F32) + sgb_ref[...].astype(F32) * yb
    mix = jnp.dot(mixin.astype(BF16), wo_ref[...], preferred_element_type=F32)
    x1 = _layer_norm(ALPHA * x_ref[...] + mix, g_ref[...], b_ref[...])
    x1_ref[...] = x1

    logits = jnp.dot(x1, rw_ref[...], preferred_element_type=F32,
                     precision=lax.Precision.HIGHEST)
    lt = jnp.transpose(logits)[:N_EXPERTS, :]
    aff = jax.nn.sigmoid(lt)
    sel = aff + rb_ref[...]
    sel_rows = [sel[e:e + 1, :] for e in range(N_EXPERTS)]
    aff_rows = [aff[e:e + 1, :] for e in range(N_EXPERTS)]
    gscore = []
    for g in range(N_GROUPS):
        s0, s1, s2, s3 = sel_rows[4 * g:4 * g + 4]
        hi1, lo1 = jnp.maximum(s0, s1), jnp.minimum(s0, s1)
        hi2, lo2 = jnp.maximum(s2, s3), jnp.minimum(s2, s3)
        top1 = jnp.maximum(hi1, hi2)
        top2 = jnp.maximum(jnp.minimum(hi1, hi2), jnp.maximum(lo1, lo2))
        gscore.append(top1 + top2)
    _, gidx = _first_max(gscore)
    within = [_pick(gidx, [sel_rows[4 * g + j] for g in range(N_GROUPS)])
              for j in range(EXPERTS_PER_GROUP)]
    awithin = [_pick(gidx, [aff_rows[4 * g + j] for g in range(N_GROUPS)])
               for j in range(EXPERTS_PER_GROUP)]
    _, i0 = _first_max(within)
    masked = [jnp.where(i0 == j, -jnp.inf, within[j]) for j in range(EXPERTS_PER_GROUP)]
    _, i1 = _first_max(masked)
    a0, a1 = _pick(i0, awithin), _pick(i1, awithin)
    denom = a0 + a1
    e0 = gidx * EXPERTS_PER_GROUP + i0
    e1 = gidx * EXPERTS_PER_GROUP + i1
    slot = lax.broadcasted_iota(jnp.int32, (TOP_K, t), 0)
    eidx_ref[...] = jnp.where(slot == 0, e0, e1)

    eiota = lax.broadcasted_iota(jnp.int32, (N_EXPERTS, t), 0)
    oh0 = jnp.where(eiota == e0, 1.0, 0.0)
    oh1 = jnp.where(eiota == e1, 1.0, 0.0)
    oh = oh0 + oh1
    row = lax.broadcasted_iota(jnp.int32, (t, t), 0)
    col = lax.broadcasted_iota(jnp.int32, (t, t), 1)
    before = jnp.where(row < col, 1.0, 0.0).astype(BF16)
    tot = base_ref[:, 0:1] + jnp.dot(oh.astype(BF16), before, preferred_element_type=F32)
    r0 = jnp.sum(oh0 * tot, axis=0, keepdims=True)
    r1 = jnp.sum(oh1 * tot, axis=0, keepdims=True)
    rank_ref[...] = jnp.where(slot == 0, r0, r1).astype(jnp.int32)
    newbase = base_ref[...] + jnp.sum(oh, axis=1, keepdims=True)
    base_ref[...] = newbase
    cnt_ref[...] = newbase

    wrow = lax.broadcasted_iota(jnp.int32, (LANES, t), 0)
    gw = jnp.where(wrow == 0, a0 / denom, jnp.where(wrow == 1, a1 / denom, 0.0))
    gwt_ref[...] = jnp.transpose(gw)


def _merge(ob, ya, gates, xf, wob, wo, g, b, rw, rb):
    m = xf.shape[0]
    t = MERGE_T
    return pl.pallas_call(
        _merge_kernel,
        grid=(m // t,),
        in_specs=[
            pl.BlockSpec((t, B_WIDTH), lambda i: (i, 0)),
            pl.BlockSpec((t, D_MODEL), lambda i: (i, 0)),
            pl.BlockSpec((t, D_MODEL), lambda i: (i, 0)),
            pl.BlockSpec((t, D_MODEL), lambda i: (i, 1)),
            pl.BlockSpec((t, D_MODEL), lambda i: (i, 0)),
            _const_spec((B_WIDTH, D_MODEL)),
            _const_spec((D_MODEL, D_MODEL)),
            _const_spec((1, D_MODEL)),
            _const_spec((1, D_MODEL)),
            _const_spec((D_MODEL, LANES)),
            _const_spec((N_EXPERTS, 1)),
        ],
        out_specs=[
            pl.BlockSpec((t, D_MODEL), lambda i: (i, 0)),
            pl.BlockSpec((TOP_K, t), lambda i: (0, i)),
            pl.BlockSpec((TOP_K, t), lambda i: (0, i)),
            pl.BlockSpec((t, LANES), lambda i: (i, 0)),
            pl.BlockSpec((N_EXPERTS, LANES), lambda i: (0, 0)),
        ],
        out_shape=[
            jax.ShapeDtypeStruct((m, D_MODEL), F32),
            jax.ShapeDtypeStruct((TOP_K, m), jnp.int32),
            jax.ShapeDtypeStruct((TOP_K, m), jnp.int32),
            jax.ShapeDtypeStruct((m, LANES), F32),
            jax.ShapeDtypeStruct((N_EXPERTS, LANES), F32),
        ],
        scratch_shapes=[pltpu.VMEM((N_EXPERTS, LANES), F32)],
        compiler_params=pltpu.CompilerParams(
            dimension_semantics=("arbitrary",), vmem_limit_bytes=VMEM_LIMIT),
        name="merge",
    )(ob, ya, gates, gates, xf, wob, wo, g, b, rw, rb)


def _dispatch_row_copy(x_ref, xd_ref, sem, src_row, dst_row):
    return pltpu.make_async_copy(x_ref.at[pl.ds(src_row, 1), :], xd_ref.at[pl.ds(dst_row, 1), :], sem)


def _dispatch_kernel(dest_ref, x_ref, xd_in_ref, xd_ref, sem):
    del xd_in_ref
    t = x_ref.shape[0]
    m = dest_ref.shape[0] // TOP_K
    base = pl.program_id(0) * t

    def issue(r, c):
        for k in range(TOP_K):
            _dispatch_row_copy(x_ref, xd_ref, sem, r, dest_ref[k * m + base + r]).start()
        return c

    lax.fori_loop(0, t, issue, 0)

    def drain(r, c):
        for k in range(TOP_K):
            _dispatch_row_copy(x_ref, xd_ref, sem, r, dest_ref[k * m + base + r]).wait()
        return c

    lax.fori_loop(0, t, drain, 0)


def _dispatch(dest, x1, p_rows):
    m, d = x1.shape
    t = DISPATCH_T
    zeros = jnp.zeros((p_rows, d), x1.dtype)
    return pl.pallas_call(
        _dispatch_kernel,
        grid_spec=pltpu.PrefetchScalarGridSpec(
            num_scalar_prefetch=1,
            grid=(m // t,),
            in_specs=[
                pl.BlockSpec((t, d), lambda i, dest: (i, 0)),
                pl.BlockSpec(memory_space=pl.ANY),
            ],
            out_specs=pl.BlockSpec(memory_space=pl.ANY),
            scratch_shapes=[pltpu.SemaphoreType.DMA(())],
        ),
        out_shape=jax.ShapeDtypeStruct((p_rows, d), x1.dtype),
        input_output_aliases={2: 0},
        compiler_params=pltpu.CompilerParams(
            dimension_semantics=("arbitrary",), vmem_limit_bytes=VMEM_LIMIT),
        name="dispatch",
    )(dest, x1, zeros)


def _expert_kernel(be_ref, nv_ref, x_ref, wg_ref, wu_ref, wd_ref, y_ref):
    del be_ref
    b = pl.program_id(0)

    @pl.when(b < nv_ref[0])
    def _():
        xb = x_ref[...].astype(BF16)
        g = jnp.dot(xb, wg_ref[...], preferred_element_type=F32)
        u = jnp.dot(xb, wu_ref[...], preferred_element_type=F32)
        h = (jax.nn.silu(g) * u).astype(BF16)
        y_ref[...] = jnp.dot(h, wd_ref[...], preferred_element_type=F32).astype(y_ref.dtype)

    @pl.when(b >= nv_ref[0])
    def _():
        y_ref[...] = jnp.zeros_like(y_ref)


def _experts(block_e, n_valid, xd, wg, wu, wd):
    p_rows, d = xd.shape
    tm = EXPERT_TM

    def x_map(b, be, nv):
        return (jnp.minimum(b, nv[0] - 1), 0)

    return pl.pallas_call(
        _expert_kernel,
        grid_spec=pltpu.PrefetchScalarGridSpec(
            num_scalar_prefetch=2,
            grid=(p_rows // tm,),
            in_specs=[
                pl.BlockSpec((tm, d), x_map),
                pl.BlockSpec((None, d, D_EXPERT), lambda b, be, nv: (be[b], 0, 0)),
                pl.BlockSpec((None, d, D_EXPERT), lambda b, be, nv: (be[b], 0, 0)),
                pl.BlockSpec((None, D_EXPERT, d), lambda b, be, nv: (be[b], 0, 0)),
            ],
            out_specs=pl.BlockSpec((tm, d), lambda b, be, nv: (b, 0)),
        ),
        out_shape=jax.ShapeDtypeStruct((p_rows, d), F32),
        compiler_params=pltpu.CompilerParams(
            dimension_semantics=("arbitrary",), vmem_limit_bytes=VMEM_LIMIT),
        name="experts",
    )(block_e, n_valid, xd, wg, wu, wd)


def _combine_row_copy(y_ref, buf_ref, sem, k, r, src_row):
    return pltpu.make_async_copy(y_ref.at[pl.ds(src_row, 1), :], buf_ref.at[k, pl.ds(r, 1), :], sem)


def _final_kernel(dest_ref, x1_ref, p_ref, gwt_ref, wpg_ref, wpp_ref, g_ref, b_ref, y_ref,
                  xo_ref, xob_ref, buf_ref, sem):
    t = x1_ref.shape[0]
    m = dest_ref.shape[0] // TOP_K
    base = pl.program_id(0) * t

    def issue(r, c):
        for k in range(TOP_K):
            _combine_row_copy(y_ref, buf_ref, sem, k, r, dest_ref[k * m + base + r]).start()
        return c

    lax.fori_loop(0, t, issue, 0)

    x1 = x1_ref[...]
    gate = jax.nn.sigmoid(jnp.dot(x1.astype(BF16), wpg_ref[...], preferred_element_type=F32))
    proj = jnp.dot(p_ref[...].astype(BF16), wpp_ref[...], preferred_element_type=F32)
    acc = ALPHA * x1 + gate * proj

    def drain(r, c):
        for k in range(TOP_K):
            _combine_row_copy(y_ref, buf_ref, sem, k, r, dest_ref[k * m + base + r]).wait()
        return c

    lax.fori_loop(0, t, drain, 0)

    gw = gwt_ref[...]
    ffn = gw[:, 0:1] * buf_ref[0] + gw[:, 1:2] * buf_ref[1]
    x2 = _layer_norm(acc + ffn, g_ref[...], b_ref[...])
    xo_ref[...] = x2
    xob_ref[...] = x2.astype(BF16)


def _final(dest, x1, p_l, gwt, wpg, wpp, g, b, y):
    m, d = x1.shape
    t = FINAL_T
    return pl.pallas_call(
        _final_kernel,
        grid_spec=pltpu.PrefetchScalarGridSpec(
            num_scalar_prefetch=1,
            grid=(m // t,),
            in_specs=[
                pl.BlockSpec((t, d), lambda i, dest: (i, 0)),
                pl.BlockSpec((t, PLE_DIM), lambda i, dest: (i, 0)),
                pl.BlockSpec((t, LANES), lambda i, dest: (i, 0)),
                pl.BlockSpec((d, d), lambda i, dest: (0, 0), pipeline_mode=pl.Buffered(1)),
                pl.BlockSpec((PLE_DIM, d), lambda i, dest: (0, 0), pipeline_mode=pl.Buffered(1)),
                pl.BlockSpec((1, d), lambda i, dest: (0, 0), pipeline_mode=pl.Buffered(1)),
                pl.BlockSpec((1, d), lambda i, dest: (0, 0), pipeline_mode=pl.Buffered(1)),
                pl.BlockSpec(memory_space=pl.ANY),
            ],
            out_specs=[
                pl.BlockSpec((t, d), lambda i, dest: (i, 0)),
                pl.BlockSpec((t, d), lambda i, dest: (i, 0)),
            ],
            scratch_shapes=[pltpu.VMEM((TOP_K, t, d), F32), pltpu.SemaphoreType.DMA(())],
        ),
        out_shape=[jax.ShapeDtypeStruct((m, d), F32), jax.ShapeDtypeStruct((m, d), BF16)],
        compiler_params=pltpu.CompilerParams(
            dimension_semantics=("arbitrary",), vmem_limit_bytes=VMEM_LIMIT),
        name="final",
    )(dest, x1, p_l, gwt, wpg, wpp, g, b, y)


def kernel(x, p, w_in, gmlp_ln_g, gmlp_ln_b, gmlp_ws, gmlp_bs, w_out_a, w_out_b, w_o, ln1_g, ln1_b,
           router_w, router_bias, exp_w_gate, exp_w_up, exp_w_down, ple_w_gate, ple_w_proj,
           ln2_g, ln2_b):
    batch, seq, d = x.shape
    m = batch * seq
    tm = EXPERT_TM
    n_blocks = (m * TOP_K) // tm + N_EXPERTS
    p_rows = n_blocks * tm

    xf = x.reshape(m, d)
    xb = xf.astype(BF16)
    p_flat = p.reshape(DEPTH, m, PLE_DIM)
    rw = jnp.pad(router_w.astype(F32), ((0, 0), (0, LANES - N_EXPERTS)))
    rb = router_bias.astype(F32).reshape(N_EXPERTS, 1)

    for i in range(DEPTH):
        ug = _inproj(xb, w_in[i], 0, 2 * A_WIDTH, jax.nn.gelu)
        qkv = _inproj(xb, w_in[i], 2 * A_WIDTH, 3 * B_WIDTH, lambda a: a,
                      first_tile_scale=B_HEAD_DIM ** -0.5 * LOG2E)
        gates = _inproj(xb, w_in[i], 2 * A_WIDTH + 3 * B_WIDTH, 2 * D_MODEL, jax.nn.sigmoid)

        bias = jnp.repeat(jnp.transpose(gmlp_bs[i]), A_HEAD_DIM, axis=1)
        ya = _gmlp(ug, gmlp_ln_g[i].reshape(1, A_WIDTH), gmlp_ln_b[i].reshape(1, A_WIDTH),
                   gmlp_ws[i], bias, w_out_a[i].astype(BF16))
        ob = _attention(qkv, batch, seq)

        x1, eidx, rank, gwt, cnt = _merge(
            ob, ya, gates, xf, w_out_b[i].astype(BF16), w_o[i].astype(BF16),
            ln1_g[i].reshape(1, d), ln1_b[i].reshape(1, d), rw, rb)

        counts = cnt[:, 0].astype(jnp.int32)
        padded = (counts + tm - 1) // tm * tm
        pends = jnp.cumsum(padded)
        pstarts = pends - padded
        eids = jnp.arange(N_EXPERTS, dtype=jnp.int32)[:, None, None]
        dest = (jnp.sum(jnp.where(eidx[None] == eids, pstarts[:, None, None], 0), axis=0)
                + rank).reshape(TOP_K * m)
        block_start = jnp.arange(n_blocks, dtype=jnp.int32) * tm
        block_e = jnp.minimum(jnp.sum((block_start[:, None] >= pends[None, :]).astype(jnp.int32), axis=1),
                              N_EXPERTS - 1)
        n_valid = (pends[-1:] // tm).astype(jnp.int32)

        xd = _dispatch(dest, x1, p_rows)
        y = _experts(block_e, n_valid, xd, exp_w_gate[i].astype(BF16), exp_w_up[i].astype(BF16),
                     exp_w_down[i].astype(BF16))
        xf, xb = _final(dest, x1, p_flat[i], gwt, ple_w_gate[i].astype(BF16),
                        ple_w_proj[i].astype(BF16), ln2_g[i].reshape(1, d), ln2_b[i].reshape(1, d), y)

    return xf.reshape(batch, seq, d)
```

```python
import functools

import jax
import jax.numpy as jnp
import numpy as np
from jax import lax
from jax.experimental import pallas as pl
from jax.experimental.pallas import tpu as pltpu

D_MODEL = 2048
DEPTH = 4
A_HEADS = 8
A_HEAD_DIM = 128
A_WIDTH = A_HEADS * A_HEAD_DIM
CHUNK = 128
B_HEADS = 8
B_HEAD_DIM = 128
B_WIDTH = B_HEADS * B_HEAD_DIM
N_EXPERTS = 16
N_GROUPS = 4
EXPERTS_PER_GROUP = N_EXPERTS // N_GROUPS
D_EXPERT = 1024
PLE_DIM = 256
ALPHA = (2 * DEPTH) ** 0.25
LN_EPS = 1e-5
LOG2E = 1.4426950408889634

LANES = 128
SUBLANES = 8
VMEM_LIMIT = 56 * 1024 * 1024

F32 = jnp.float32
BF16 = jnp.bfloat16

PAIRS = [(lo, hi) for lo in range(EXPERTS_PER_GROUP) for hi in range(lo + 1, EXPERTS_PER_GROUP)]
PAIRS_PER_GROUP = len(PAIRS)
N_BUCKETS = N_GROUPS * PAIRS_PER_GROUP
BUCKET_ROWS = 32
BUCKET_LO = np.array([g * EXPERTS_PER_GROUP + lo for g in range(N_GROUPS) for lo, _ in PAIRS], np.int32)
BUCKET_HI = np.array([g * EXPERTS_PER_GROUP + hi for g in range(N_GROUPS) for _, hi in PAIRS], np.int32)
PAIR_BASE = [PAIRS.index((lo, lo + 1)) for lo in range(EXPERTS_PER_GROUP - 1)]
XD_WIDTH = D_MODEL + LANES

INPROJ_TM = 1024
INPROJ_TN = 1024
GMLP_T = 512
ATTN_T = 256
ATTN_HEADS_PER_STEP = 4
MERGE_T = 256
DISPATCH_T = 256
EXPERT_TM = 128
FINAL_T = 256


def _layer_norm(xf, gain, bias):
    mu = jnp.mean(xf, axis=-1, keepdims=True)
    xc = xf - mu
    var = jnp.mean(xc * xc, axis=-1, keepdims=True)
    return xc * lax.rsqrt(var + LN_EPS) * gain + bias


def _const_spec(shape):
    nd = len(shape)
    return pl.BlockSpec(shape, lambda *_: (0,) * nd, pipeline_mode=pl.Buffered(1))


def _inproj_kernel(x_ref, w_ref, o_ref, wb_ref, *, act, first_tile_scale):
    @pl.when(pl.program_id(1) == 0)
    def _():
        wb_ref[...] = w_ref[...].astype(BF16)

    y = act(jnp.dot(x_ref[...], wb_ref[...], preferred_element_type=F32))
    if first_tile_scale is not None:
        y = y * jnp.where(pl.program_id(0) == 0, first_tile_scale, 1.0)
    o_ref[...] = y.astype(o_ref.dtype)


def _inproj(xb, w, layer, col0, ncols, act, first_tile_scale=None):
    m, k = xb.shape
    tm, tn = INPROJ_TM, INPROJ_TN
    cb0 = col0 // tn
    return pl.pallas_call(
        functools.partial(_inproj_kernel, act=act, first_tile_scale=first_tile_scale),
        grid=(ncols // tn, m // tm),
        in_specs=[
            pl.BlockSpec((tm, k), lambda j, i: (i, 0)),
            pl.BlockSpec((None, k, tn), lambda j, i: (layer, 0, cb0 + j), pipeline_mode=pl.Buffered(1)),
        ],
        out_specs=pl.BlockSpec((tm, tn), lambda j, i: (i, j)),
        out_shape=jax.ShapeDtypeStruct((m, ncols), BF16),
        scratch_shapes=[pltpu.VMEM((k, tn), BF16)],
        compiler_params=pltpu.CompilerParams(
            dimension_semantics=("arbitrary", "arbitrary"), vmem_limit_bytes=VMEM_LIMIT),
        name="inproj",
    )(xb, w)


def _gmlp_kernel(u_ref, gv_ref, lng_ref, lnb_ref, ws_ref, bias_ref, wo_ref, o_ref, yin_ref):
    t = u_ref.shape[0]
    nc = t // CHUNK
    v = _layer_norm(gv_ref[...].astype(F32), lng_ref[...], lnb_ref[...]).astype(BF16)
    row = lax.broadcasted_iota(jnp.int32, (CHUNK, CHUNK), 0)
    col = lax.broadcasted_iota(jnp.int32, (CHUNK, CHUNK), 1)
    tril = row >= col
    for h in range(A_HEADS):
        hs = slice(h * A_HEAD_DIM, (h + 1) * A_HEAD_DIM)
        wsm = jnp.where(tril, ws_ref[h], 0.0).astype(BF16)
        vh = jnp.concatenate([v[c * CHUNK:(c + 1) * CHUNK, hs] for c in range(nc)], axis=1)
        f = jnp.dot(wsm, vh, preferred_element_type=F32)
        bh = bias_ref[:, hs]
        for c in range(nc):
            rs = slice(c * CHUNK, (c + 1) * CHUNK)
            fc = f[:, c * A_HEAD_DIM:(c + 1) * A_HEAD_DIM] + bh
            yin_ref[rs, hs] = (u_ref[rs, hs].astype(F32) * fc).astype(BF16)
    o_ref[...] = jnp.dot(yin_ref[...], wo_ref[...], preferred_element_type=F32).astype(o_ref.dtype)


def _gmlp(ug, lng, lnb, ws, layer, bias, wo):
    m = ug.shape[0]
    t = GMLP_T
    return pl.pallas_call(
        _gmlp_kernel,
        grid=(m // t,),
        in_specs=[
            pl.BlockSpec((t, A_WIDTH), lambda i: (i, 0)),
            pl.BlockSpec((t, A_WIDTH), lambda i: (i, 1)),
            _const_spec((1, A_WIDTH)),
            _const_spec((1, A_WIDTH)),
            pl.BlockSpec((None, A_HEADS, CHUNK, CHUNK), lambda i: (layer, 0, 0, 0),
                         pipeline_mode=pl.Buffered(1)),
            _const_spec((CHUNK, A_WIDTH)),
            _const_spec((A_WIDTH, D_MODEL)),
        ],
        out_specs=pl.BlockSpec((t, D_MODEL), lambda i: (i, 0)),
        out_shape=jax.ShapeDtypeStruct((m, D_MODEL), BF16),
        scratch_shapes=[pltpu.VMEM((t, A_WIDTH), BF16)],
        compiler_params=pltpu.CompilerParams(
            dimension_semantics=("parallel",), vmem_limit_bytes=VMEM_LIMIT),
        name="gmlp",
    )(ug, ug, lng, lnb, ws, bias, wo)


def _attn_kernel(q_ref, k_ref, v_ref, o_ref):
    t = q_ref.shape[0]
    nh = q_ref.shape[1] // B_HEAD_DIM
    heads = range(nh)
    i = pl.program_id(2)
    row = lax.broadcasted_iota(jnp.int32, (t, t), 0)
    col = lax.broadcasted_iota(jnp.int32, (t, t), 1)
    neg_upper = jnp.where(row > col, -1.0, 0.0).astype(BF16)
    causal = col < row
    hs = [slice(h * B_HEAD_DIM, (h + 1) * B_HEAD_DIM) for h in heads]
    qs = [q_ref[:, s] for s in hs]

    def tiles(off, state, mask):
        z = [lax.dot_general(qs[h], k_ref[pl.ds(off, t), hs[h]], (((1,), (1,)), ((), ())),
                             preferred_element_type=F32) for h in heads]
        sp, logsig, packed = [], [], []
        for h in heads:
            neg_abs = pltpu.bitcast(pltpu.bitcast(z[h], jnp.uint32) | jnp.uint32(0x80000000), F32)
            s = jnp.maximum(z[h], 0.0) + jnp.log(1.0 + jnp.exp2(neg_abs)) * LOG2E
            logsig.append(z[h] - s)
            if mask is not None:
                s = jnp.where(mask, s, 0.0)
            sp.append(s)
            packed.append(s.astype(BF16))
        suffix = [jnp.dot(packed[h], neg_upper, preferred_element_type=F32) for h in heads]
        a = []
        for h in heads:
            ah = jnp.exp2(logsig[h] + suffix[h] + state[h][0])
            if mask is not None:
                ah = jnp.where(mask, ah, 0.0)
            a.append(ah.astype(BF16))
        return tuple(
            (state[h][0] - jnp.sum(sp[h], axis=1, keepdims=True),
             state[h][1] + jnp.dot(a[h], v_ref[pl.ds(off, t), hs[h]], preferred_element_type=F32))
            for h in heads)

    init = tuple((jnp.zeros((t, 1), F32), jnp.zeros((t, B_HEAD_DIM), F32)) for _ in range(nh))
    state = tiles(pl.multiple_of(i * t, t), init, causal)

    def body(jj, st):
        return tiles(pl.multiple_of((i - 1 - jj) * t, t), st, None)

    state = lax.fori_loop(0, i, body, state)
    for h in range(nh):
        o_ref[:, hs[h]] = state[h][1].astype(o_ref.dtype)


def _attention(qkv, batch, seq):
    m = qkv.shape[0]
    t = ATTN_T
    nq = seq // t
    w = ATTN_HEADS_PER_STEP * B_HEAD_DIM
    hb = B_WIDTH // w
    return pl.pallas_call(
        _attn_kernel,
        grid=(batch, hb, nq),
        in_specs=[
            pl.BlockSpec((t, w), lambda b, h, i: (b * nq + i, h)),
            pl.BlockSpec((seq, w), lambda b, h, i: (b, hb + h)),
            pl.BlockSpec((seq, w), lambda b, h, i: (b, 2 * hb + h)),
        ],
        out_specs=pl.BlockSpec((t, w), lambda b, h, i: (b * nq + i, h)),
        out_shape=jax.ShapeDtypeStruct((m, B_WIDTH), BF16),
        compiler_params=pltpu.CompilerParams(
            dimension_semantics=("parallel", "parallel", "parallel"), vmem_limit_bytes=VMEM_LIMIT),
        name="attn",
    )(qkv, qkv, qkv)


def _first_max(vals):
    best, idx = vals[0], jnp.zeros(vals[0].shape, jnp.int32)
    for j in range(1, len(vals)):
        better = vals[j] > best
        best = jnp.where(better, vals[j], best)
        idx = jnp.where(better, j, idx)
    return best, idx


def _pick(idx, vals):
    out = vals[0]
    for j in range(1, len(vals)):
        out = jnp.where(idx == j, vals[j], out)
    return out


def _merge_kernel(ob_ref, ya_ref, sga_ref, sgb_ref, x_ref, wob_ref, wo_ref, g_ref, b_ref,
                  rwh_ref, rwl_ref, rb_ref, x1_ref, br_ref, gwt_ref, cnt_ref, base_ref):
    t = x_ref.shape[0]

    @pl.when(pl.program_id(0) == 0)
    def _():
        base_ref[...] = jnp.zeros_like(base_ref)

    yb = jnp.dot(ob_ref[...], wob_ref[...], preferred_element_type=F32)
    mixin = sga_ref[...].astype(F32) * ya_ref[...].astype(F32) + sgb_ref[...].astype(F32) * yb
    mix = jnp.dot(mixin.astype(BF16), wo_ref[...], preferred_element_type=F32)
    x1 = _layer_norm(ALPHA * x_ref[...] + mix, g_ref[...], b_ref[...])
    x1_ref[...] = x1

    x_hi = x1.astype(BF16)
    x_lo = (x1 - x_hi.astype(F32)).astype(BF16)
    logits = (jnp.dot(x_hi, rwh_ref[...], preferred_element_type=F32)
              + jnp.dot(x_lo, rwh_ref[...], preferred_element_type=F32)
              + jnp.dot(x_hi, rwl_ref[...], preferred_element_type=F32))
    lt = jnp.transpose(logits)[:N_EXPERTS, :]
    aff = jax.nn.sigmoid(lt)
    sel = aff + rb_ref[...]
    sel_rows = [sel[e:e + 1, :] for e in range(N_EXPERTS)]
    aff_rows = [aff[e:e + 1, :] for e in range(N_EXPERTS)]
    gscore = []
    for g in range(N_GROUPS):
        s0, s1, s2, s3 = sel_rows[4 * g:4 * g + 4]
        hi1, lo1 = jnp.maximum(s0, s1), jnp.minimum(s0, s1)
        hi2, lo2 = jnp.maximum(s2, s3), jnp.minimum(s2, s3)
        top1 = jnp.maximum(hi1, hi2)
        top2 = jnp.maximum(jnp.minimum(hi1, hi2), jnp.maximum(lo1, lo2))
        gscore.append(top1 + top2)
    _, gidx = _first_max(gscore)
    within = [_pick(gidx, [sel_rows[4 * g + j] for g in range(N_GROUPS)])
              for j in range(EXPERTS_PER_GROUP)]
    awithin = [_pick(gidx, [aff_rows[4 * g + j] for g in range(N_GROUPS)])
               for j in range(EXPERTS_PER_GROUP)]
    _, i0 = _first_max(within)
    masked = [jnp.where(i0 == j, -jnp.inf, within[j]) for j in range(EXPERTS_PER_GROUP)]
    _, i1 = _first_max(masked)
    a0, a1 = _pick(i0, awithin), _pick(i1, awithin)
    denom = a0 + a1
    w0, w1 = a0 / denom, a1 / denom

    swap = i1 < i0
    lo = jnp.where(swap, i1, i0)
    hi = jnp.where(swap, i0, i1)
    w_lo = jnp.where(swap, w1, w0)
    w_hi = jnp.where(swap, w0, w1)
    pair = _pick(lo, PAIR_BASE) + (hi - lo - 1)
    bucket = gidx * PAIRS_PER_GROUP + pair

    biota = lax.broadcasted_iota(jnp.int32, (BUCKET_ROWS, t), 0)
    oh = jnp.where(biota == bucket, 1.0, 0.0)
    row = lax.broadcasted_iota(jnp.int32, (t, t), 0)
    col = lax.broadcasted_iota(jnp.int32, (t, t), 1)
    before = jnp.where(row < col, 1.0, 0.0).astype(BF16)
    tot = base_ref[:, 0:1] + jnp.dot(oh.astype(BF16), before, preferred_element_type=F32)
    rank = jnp.sum(oh * tot, axis=0, keepdims=True)
    slot = lax.broadcasted_iota(jnp.int32, (2, t), 0)
    br_ref[...] = jnp.where(slot == 0, bucket, rank.astype(jnp.int32))
    newbase = base_ref[...] + jnp.sum(oh, axis=1, keepdims=True)
    base_ref[...] = newbase
    cnt_ref[...] = newbase

    wrow = lax.broadcasted_iota(jnp.int32, (LANES, t), 0)
    gw = jnp.where(wrow == 0, w_lo, jnp.where(wrow == 1, w_hi, 0.0))
    gwt_ref[...] = jnp.transpose(gw)


def _merge(ob, ya, gates, xf, wob, wo, g, b, rwh, rwl, rb):
    m = xf.shape[0]
    t = MERGE_T
    return pl.pallas_call(
        _merge_kernel,
        grid=(m // t,),
        in_specs=[
            pl.BlockSpec((t, B_WIDTH), lambda i: (i, 0)),
            pl.BlockSpec((t, D_MODEL), lambda i: (i, 0)),
            pl.BlockSpec((t, D_MODEL), lambda i: (i, 0)),
            pl.BlockSpec((t, D_MODEL), lambda i: (i, 1)),
            pl.BlockSpec((t, D_MODEL), lambda i: (i, 0)),
            _const_spec((B_WIDTH, D_MODEL)),
            _const_spec((D_MODEL, D_MODEL)),
            _const_spec((1, D_MODEL)),
            _const_spec((1, D_MODEL)),
            _const_spec((D_MODEL, LANES)),
            _const_spec((D_MODEL, LANES)),
            _const_spec((N_EXPERTS, 1)),
        ],
        out_specs=[
            pl.BlockSpec((t, D_MODEL), lambda i: (i, 0)),
            pl.BlockSpec((2, t), lambda i: (0, i)),
            pl.BlockSpec((t, LANES), lambda i: (i, 0)),
            pl.BlockSpec((BUCKET_ROWS, LANES), lambda i: (0, 0)),
        ],
        out_shape=[
            jax.ShapeDtypeStruct((m, D_MODEL), F32),
            jax.ShapeDtypeStruct((2, m), jnp.int32),
            jax.ShapeDtypeStruct((m, LANES), F32),
            jax.ShapeDtypeStruct((BUCKET_ROWS, LANES), F32),
        ],
        scratch_shapes=[pltpu.VMEM((BUCKET_ROWS, LANES), F32)],
        compiler_params=pltpu.CompilerParams(
            dimension_semantics=("arbitrary",), vmem_limit_bytes=VMEM_LIMIT),
        name="merge",
    )(ob, ya, gates, gates, xf, wob, wo, g, b, rwh, rwl, rb)


def _dispatch_row_copy(xext_ref, xd_ref, sem, src_row, dst_row):
    return pltpu.make_async_copy(xext_ref.at[pl.ds(src_row, 1), :], xd_ref.at[pl.ds(dst_row, 1), :], sem)


def _zero_fill_copy(zero_ref, xd_ref, sem, dst_row):
    rows = zero_ref.shape[0]
    return pltpu.make_async_copy(zero_ref, xd_ref.at[pl.ds(pl.multiple_of(dst_row, SUBLANES), rows), :], sem)


def _ple_dispatch_kernel(dest_ref, zstart_ref, x1_ref, p_ref, gwt_ref, wpg_ref, wpp_ref,
                         base_ref, xd_ref, xext_ref, zero_ref, sem, zsem):
    t = x1_ref.shape[0]
    first = pl.program_id(0) * t

    @pl.when(pl.program_id(0) == 0)
    def _():
        zero_ref[...] = jnp.zeros_like(zero_ref)
        for k in range(N_BUCKETS):
            _zero_fill_copy(zero_ref, xd_ref, zsem, zstart_ref[k]).start()
        for k in range(N_BUCKETS):
            _zero_fill_copy(zero_ref, xd_ref, zsem, zstart_ref[k]).wait()

    x1 = x1_ref[...]
    xext_ref[:, :D_MODEL] = x1
    xext_ref[:, D_MODEL:] = gwt_ref[...]

    for r in range(t):
        _dispatch_row_copy(xext_ref, xd_ref, sem, r, dest_ref[first + r]).start()

    gate = jax.nn.sigmoid(jnp.dot(x1.astype(BF16), wpg_ref[...], preferred_element_type=F32))
    proj = jnp.dot(p_ref[...].astype(BF16), wpp_ref[...], preferred_element_type=F32)
    base_ref[...] = ALPHA * x1 + gate * proj

    for r in range(t):
        _dispatch_row_copy(xext_ref, xd_ref, sem, 0, 0).wait()


def _ple_dispatch(dest, zstart, x1, p, layer, gwt, wpg, wpp, xd_rows):
    m, d = x1.shape
    t = DISPATCH_T
    return pl.pallas_call(
        _ple_dispatch_kernel,
        grid_spec=pltpu.PrefetchScalarGridSpec(
            num_scalar_prefetch=2,
            grid=(m // t,),
            in_specs=[
                pl.BlockSpec((t, d), lambda i, *_: (i, 0)),
                pl.BlockSpec((None, t, PLE_DIM), lambda i, *_: (layer, i, 0)),
                pl.BlockSpec((t, LANES), lambda i, *_: (i, 0)),
                pl.BlockSpec((d, d), lambda i, *_: (0, 0), pipeline_mode=pl.Buffered(1)),
                pl.BlockSpec((PLE_DIM, d), lambda i, *_: (0, 0), pipeline_mode=pl.Buffered(1)),
            ],
            out_specs=[
                pl.BlockSpec((t, d), lambda i, *_: (i, 0)),
                pl.BlockSpec(memory_space=pl.ANY),
            ],
            scratch_shapes=[
                pltpu.VMEM((t, XD_WIDTH), F32),
                pltpu.VMEM((EXPERT_TM, XD_WIDTH), F32),
                pltpu.SemaphoreType.DMA(()),
                pltpu.SemaphoreType.DMA(()),
            ],
        ),
        out_shape=[jax.ShapeDtypeStruct((m, d), F32), jax.ShapeDtypeStruct((xd_rows, XD_WIDTH), F32)],
        compiler_params=pltpu.CompilerParams(
            dimension_semantics=("arbitrary",), vmem_limit_bytes=VMEM_LIMIT),
        name="ple_dispatch",
    )(dest, zstart, x1, p, gwt, wpg, wpp)


def _ffn(xb, wg_ref, wu_ref, wd_ref):
    g = jnp.dot(xb, wg_ref[...], preferred_element_type=F32)
    u = jnp.dot(xb, wu_ref[...], preferred_element_type=F32)
    h = (jax.nn.silu(g) * u).astype(BF16)
    return jnp.dot(h, wd_ref[...], preferred_element_type=F32)


def _expert_kernel(lo_ref, hi_ref, nv_ref, x_ref, wgl_ref, wul_ref, wdl_ref, wgh_ref, wuh_ref, wdh_ref,
                   y_ref):
    del lo_ref, hi_ref
    b = pl.program_id(0)

    @pl.when(b < nv_ref[0])
    def _():
        xb = x_ref[:, :D_MODEL].astype(BF16)
        w_lo = x_ref[:, D_MODEL:D_MODEL + 1]
        w_hi = x_ref[:, D_MODEL + 1:D_MODEL + 2]
        y_ref[...] = (w_lo * _ffn(xb, wgl_ref, wul_ref, wdl_ref)
                      + w_hi * _ffn(xb, wgh_ref, wuh_ref, wdh_ref))

    @pl.when(b >= nv_ref[0])
    def _():
        y_ref[...] = jnp.zeros_like(y_ref)


def _experts(blk_lo, blk_hi, n_valid, xd, wg, wu, wd, n_blocks):
    d = D_MODEL
    tm = EXPERT_TM

    def x_map(b, lo, hi, nv):
        return (jnp.minimum(b, jnp.maximum(nv[0] - 1, 0)), 0)

    def lo_map(b, lo, hi, nv):
        return (lo[b], 0, 0)

    def hi_map(b, lo, hi, nv):
        return (hi[b], 0, 0)

    return pl.pallas_call(
        _expert_kernel,
        grid_spec=pltpu.PrefetchScalarGridSpec(
            num_scalar_prefetch=3,
            grid=(n_blocks,),
            in_specs=[
                pl.BlockSpec((tm, XD_WIDTH), x_map),
                pl.BlockSpec((None, d, D_EXPERT), lo_map, pipeline_mode=pl.Buffered(1)),
                pl.BlockSpec((None, d, D_EXPERT), lo_map, pipeline_mode=pl.Buffered(1)),
                pl.BlockSpec((None, D_EXPERT, d), lo_map, pipeline_mode=pl.Buffered(1)),
                pl.BlockSpec((None, d, D_EXPERT), hi_map),
                pl.BlockSpec((None, d, D_EXPERT), hi_map),
                pl.BlockSpec((None, D_EXPERT, d), hi_map),
            ],
            out_specs=pl.BlockSpec((tm, d), lambda b, lo, hi, nv: (b, 0)),
        ),
        out_shape=jax.ShapeDtypeStruct((n_blocks * tm, d), F32),
        compiler_params=pltpu.CompilerParams(
            dimension_semantics=("arbitrary",), vmem_limit_bytes=VMEM_LIMIT),
        name="experts",
    )(blk_lo, blk_hi, n_valid, xd, wg, wu, wd, wg, wu, wd)


def _combine_row_copy(y_ref, buf_ref, sems, slot, r, src_row):
    return pltpu.make_async_copy(y_ref.at[pl.ds(src_row, 1), :], buf_ref.at[slot, pl.ds(r, 1), :],
                                 sems.at[slot])


def _final_kernel(dest_ref, base_ref, g_ref, b_ref, y_ref, xo_ref, xob_ref, buf_ref, sems):
    t = base_ref.shape[0]
    i = pl.program_id(0)

    def issue(tile, slot):
        for r in range(t):
            _combine_row_copy(y_ref, buf_ref, sems, slot, r, dest_ref[tile * t + r]).start()

    @pl.when(i == 0)
    def _():
        issue(0, 0)

    nxt = i + 1
    for s in range(2):
        @pl.when((nxt < pl.num_programs(0)) & (nxt % 2 == s))
        def _():
            issue(nxt, s)

    for s in range(2):
        @pl.when(i % 2 == s)
        def _():
            for r in range(t):
                _combine_row_copy(y_ref, buf_ref, sems, s, 0, 0).wait()

    x2 = _layer_norm(base_ref[...] + buf_ref[i % 2], g_ref[...], b_ref[...])
    xo_ref[...] = x2
    xob_ref[...] = x2.astype(BF16)


def _final(dest, base, g, b, y):
    m, d = base.shape
    t = FINAL_T
    return pl.pallas_call(
        _final_kernel,
        grid_spec=pltpu.PrefetchScalarGridSpec(
            num_scalar_prefetch=1,
            grid=(m // t,),
            in_specs=[
                pl.BlockSpec((t, d), lambda i, dest: (i, 0)),
                pl.BlockSpec((1, d), lambda i, dest: (0, 0), pipeline_mode=pl.Buffered(1)),
                pl.BlockSpec((1, d), lambda i, dest: (0, 0), pipeline_mode=pl.Buffered(1)),
                pl.BlockSpec(memory_space=pl.ANY),
            ],
            out_specs=[
                pl.BlockSpec((t, d), lambda i, dest: (i, 0)),
                pl.BlockSpec((t, d), lambda i, dest: (i, 0)),
            ],
            scratch_shapes=[pltpu.VMEM((2, t, d), F32), pltpu.SemaphoreType.DMA((2,))],
        ),
        out_shape=[jax.ShapeDtypeStruct((m, d), F32), jax.ShapeDtypeStruct((m, d), BF16)],
        compiler_params=pltpu.CompilerParams(
            dimension_semantics=("arbitrary",), vmem_limit_bytes=VMEM_LIMIT),
        name="final",
    )(dest, base, g, b, y)


def kernel(x, p, w_in, gmlp_ln_g, gmlp_ln_b, gmlp_ws, gmlp_bs, w_out_a, w_out_b, w_o, ln1_g, ln1_b,
           router_w, router_bias, exp_w_gate, exp_w_up, exp_w_down, ple_w_gate, ple_w_proj,
           ln2_g, ln2_b):
    batch, seq, d = x.shape
    m = batch * seq
    tm = EXPERT_TM
    n_blocks = m // tm + N_BUCKETS
    xd_rows = (n_blocks + 1) * tm

    xf = x.reshape(m, d)
    xb = xf.astype(BF16)
    p_flat = p.reshape(DEPTH, m, PLE_DIM)
    rw = jnp.pad(router_w.astype(F32), ((0, 0), (0, LANES - N_EXPERTS)))
    rwh = rw.astype(BF16)
    rwl = (rw - rwh.astype(F32)).astype(BF16)
    rb = router_bias.astype(F32).reshape(N_EXPERTS, 1)
    bucket_ids = jnp.arange(N_BUCKETS, dtype=jnp.int32)
    block_ids = jnp.arange(n_blocks, dtype=jnp.int32)

    for i in range(DEPTH):
        ug = _inproj(xb, w_in, i, 0, 2 * A_WIDTH, jax.nn.gelu)
        qkv = _inproj(xb, w_in, i, 2 * A_WIDTH, 3 * B_WIDTH, lambda a: a,
                      first_tile_scale=B_HEAD_DIM ** -0.5 * LOG2E)
        gates = _inproj(xb, w_in, i, 2 * A_WIDTH + 3 * B_WIDTH, 2 * D_MODEL, jax.nn.sigmoid)

        bias = jnp.repeat(jnp.transpose(gmlp_bs[i]), A_HEAD_DIM, axis=1)
        ya = _gmlp(ug, gmlp_ln_g[i].reshape(1, A_WIDTH), gmlp_ln_b[i].reshape(1, A_WIDTH),
                   gmlp_ws, i, bias, w_out_a[i].astype(BF16))
        ob = _attention(qkv, batch, seq)

        x1, br, gwt, cnt = _merge(
            ob, ya, gates, xf, w_out_b[i].astype(BF16), w_o[i].astype(BF16),
            ln1_g[i].reshape(1, d), ln1_b[i].reshape(1, d), rwh, rwl, rb)

        counts = cnt[:N_BUCKETS, 0].astype(jnp.int32)
        nblk = jnp.maximum((counts + tm - 1) // tm, 1)
        blk_end = jnp.cumsum(nblk)
        pstart = (blk_end - nblk) * tm
        dest = jnp.sum(jnp.where(br[0][None, :] == bucket_ids[:, None], pstart[:, None], 0), axis=0) + br[1]
        zstart = pstart + counts // SUBLANES * SUBLANES
        blk_bucket = jnp.minimum(
            jnp.sum((block_ids[:, None] >= blk_end[None, :]).astype(jnp.int32), axis=1), N_BUCKETS - 1)
        blk_lo = jnp.asarray(BUCKET_LO)[blk_bucket]
        blk_hi = jnp.asarray(BUCKET_HI)[blk_bucket]
        n_valid = blk_end[-1:].astype(jnp.int32)

        base, xd = _ple_dispatch(dest, zstart, x1, p_flat, i, gwt, ple_w_gate[i].astype(BF16),
                                 ple_w_proj[i].astype(BF16), xd_rows)
        y = _experts(blk_lo, blk_hi, n_valid, xd, exp_w_gate[i].astype(BF16), exp_w_up[i].astype(BF16),
                     exp_w_down[i].astype(BF16), n_blocks)
        xf, xb = _final(dest, base, ln2_g[i].reshape(1, d), ln2_b[i].reshape(1, d), y)

    return xf.reshape(batch, seq, d)
```

```python
import functools

import jax
import jax.numpy as jnp
import numpy as np
from jax import lax
from jax.experimental import pallas as pl
from jax.experimental.pallas import tpu as pltpu

D_MODEL = 2048
DEPTH = 4
A_HEADS = 8
A_HEAD_DIM = 128
A_WIDTH = A_HEADS * A_HEAD_DIM
CHUNK = 128
B_HEADS = 8
B_HEAD_DIM = 128
B_WIDTH = B_HEADS * B_HEAD_DIM
N_EXPERTS = 16
N_GROUPS = 4
EXPERTS_PER_GROUP = N_EXPERTS // N_GROUPS
D_EXPERT = 1024
PLE_DIM = 256
ALPHA = (2 * DEPTH) ** 0.25
LN_EPS = 1e-5
LOG2E = 1.4426950408889634

LANES = 128
SUBLANES = 8
VMEM_LIMIT = 56 * 1024 * 1024
EXPERT_VMEM_LIMIT = 62 * 1024 * 1024

F32 = jnp.float32
BF16 = jnp.bfloat16

PAIRS = [(lo, hi) for lo in range(EXPERTS_PER_GROUP) for hi in range(lo + 1, EXPERTS_PER_GROUP)]
PAIRS_PER_GROUP = len(PAIRS)
N_BUCKETS = N_GROUPS * PAIRS_PER_GROUP
BUCKET_ROWS = 32
BUCKET_LO = np.array([g * EXPERTS_PER_GROUP + lo for g in range(N_GROUPS) for lo, _ in PAIRS], np.int32)
BUCKET_HI = np.array([g * EXPERTS_PER_GROUP + hi for g in range(N_GROUPS) for _, hi in PAIRS], np.int32)
PAIR_BASE = [PAIRS.index((lo, lo + 1)) for lo in range(EXPERTS_PER_GROUP - 1)]
XD_WIDTH = D_MODEL + LANES

INPROJ_TM = 1024
INPROJ_TN = 1024
INPROJ_CHUNKS = 4
GMLP_T = 512
ATTN_T = 256
ATTN_HEADS_PER_STEP = 4
MERGE_T = 256
DISPATCH_T = 256
EXPERT_TM = 128
FINAL_T = 256


def _layer_norm(xf, gain, bias):
    mu = jnp.mean(xf, axis=-1, keepdims=True)
    xc = xf - mu
    var = jnp.mean(xc * xc, axis=-1, keepdims=True)
    return xc * lax.rsqrt(var + LN_EPS) * gain + bias


def _const_spec(shape):
    nd = len(shape)
    return pl.BlockSpec(shape, lambda *_: (0,) * nd, pipeline_mode=pl.Buffered(1))


def _layer_spec(shape, layer):
    nd = len(shape)
    return pl.BlockSpec((None,) + tuple(shape), lambda *_: (layer,) + (0,) * nd,
                        pipeline_mode=pl.Buffered(1))


def _inproj_kernel(x_ref, w_ref, o_ref, wb_ref, *, act, first_tile_scale):
    @pl.when(pl.program_id(1) == 0)
    def _():
        wb_ref[...] = w_ref[...].astype(BF16)

    rows = x_ref.shape[0] // INPROJ_CHUNKS
    for c in range(INPROJ_CHUNKS):
        rs = slice(c * rows, (c + 1) * rows)
        y = act(jnp.dot(x_ref[rs, :], wb_ref[...], preferred_element_type=F32))
        if first_tile_scale is not None:
            y = y * jnp.where(pl.program_id(0) == 0, first_tile_scale, 1.0)
        o_ref[rs, :] = y.astype(o_ref.dtype)


def _inproj(xb, w, layer, col0, ncols, act, first_tile_scale=None):
    m, k = xb.shape
    tm, tn = INPROJ_TM, INPROJ_TN
    cb0 = col0 // tn
    return pl.pallas_call(
        functools.partial(_inproj_kernel, act=act, first_tile_scale=first_tile_scale),
        grid=(ncols // tn, m // tm),
        in_specs=[
            pl.BlockSpec((tm, k), lambda j, i: (i, 0)),
            pl.BlockSpec((None, k, tn), lambda j, i: (layer, 0, cb0 + j), pipeline_mode=pl.Buffered(1)),
        ],
        out_specs=pl.BlockSpec((tm, tn), lambda j, i: (i, j)),
        out_shape=jax.ShapeDtypeStruct((m, ncols), BF16),
        scratch_shapes=[pltpu.VMEM((k, tn), BF16)],
        compiler_params=pltpu.CompilerParams(
            dimension_semantics=("arbitrary", "arbitrary"), vmem_limit_bytes=VMEM_LIMIT),
        name="inproj",
    )(xb, w)


def _gmlp_kernel(u_ref, gv_ref, lng_ref, lnb_ref, ws_ref, bias_ref, wo_ref, o_ref, yin_ref):
    t = u_ref.shape[0]
    nc = t // CHUNK
    v = _layer_norm(gv_ref[...].astype(F32), lng_ref[...], lnb_ref[...]).astype(BF16)
    row = lax.broadcasted_iota(jnp.int32, (CHUNK, CHUNK), 0)
    col = lax.broadcasted_iota(jnp.int32, (CHUNK, CHUNK), 1)
    tril = row >= col
    for h in range(A_HEADS):
        hs = slice(h * A_HEAD_DIM, (h + 1) * A_HEAD_DIM)
        wsm = jnp.where(tril, ws_ref[h], 0.0).astype(BF16)
        vh = jnp.concatenate([v[c * CHUNK:(c + 1) * CHUNK, hs] for c in range(nc)], axis=1)
        f = jnp.dot(wsm, vh, preferred_element_type=F32)
        bh = bias_ref[:, hs]
        for c in range(nc):
            rs = slice(c * CHUNK, (c + 1) * CHUNK)
            fc = f[:, c * A_HEAD_DIM:(c + 1) * A_HEAD_DIM] + bh
            yin_ref[rs, hs] = (u_ref[rs, hs].astype(F32) * fc).astype(BF16)
    o_ref[...] = jnp.dot(yin_ref[...], wo_ref[...], preferred_element_type=F32).astype(o_ref.dtype)


def _gmlp(ug, lng, lnb, ws, layer, bias, wo):
    m = ug.shape[0]
    t = GMLP_T
    return pl.pallas_call(
        _gmlp_kernel,
        grid=(m // t,),
        in_specs=[
            pl.BlockSpec((t, A_WIDTH), lambda i: (i, 0)),
            pl.BlockSpec((t, A_WIDTH), lambda i: (i, 1)),
            _const_spec((1, A_WIDTH)),
            _const_spec((1, A_WIDTH)),
            _layer_spec((A_HEADS, CHUNK, CHUNK), layer),
            _const_spec((CHUNK, A_WIDTH)),
            _layer_spec((A_WIDTH, D_MODEL), layer),
        ],
        out_specs=pl.BlockSpec((t, D_MODEL), lambda i: (i, 0)),
        out_shape=jax.ShapeDtypeStruct((m, D_MODEL), BF16),
        scratch_shapes=[pltpu.VMEM((t, A_WIDTH), BF16)],
        compiler_params=pltpu.CompilerParams(
            dimension_semantics=("parallel",), vmem_limit_bytes=VMEM_LIMIT),
        name="gmlp",
    )(ug, ug, lng, lnb, ws, bias, wo)


def _attn_kernel(q_ref, k_ref, v_ref, o_ref):
    t = q_ref.shape[0]
    nh = q_ref.shape[1] // B_HEAD_DIM
    heads = range(nh)
    i = pl.program_id(2)
    row = lax.broadcasted_iota(jnp.int32, (t, t), 0)
    col = lax.broadcasted_iota(jnp.int32, (t, t), 1)
    neg_upper = jnp.where(row > col, -1.0, 0.0).astype(BF16)
    causal = col < row
    hs = [slice(h * B_HEAD_DIM, (h + 1) * B_HEAD_DIM) for h in heads]
    qs = [q_ref[:, s] for s in hs]

    def tiles(off, state, mask):
        z = [lax.dot_general(qs[h], k_ref[pl.ds(off, t), hs[h]], (((1,), (1,)), ((), ())),
                             preferred_element_type=F32) for h in heads]
        sp, logsig, packed = [], [], []
        for h in heads:
            neg_abs = pltpu.bitcast(pltpu.bitcast(z[h], jnp.uint32) | jnp.uint32(0x80000000), F32)
            s = jnp.maximum(z[h], 0.0) + jnp.log(1.0 + jnp.exp2(neg_abs)) * LOG2E
            logsig.append(z[h] - s)
            if mask is not None:
                s = jnp.where(mask, s, 0.0)
            sp.append(s)
            packed.append(s.astype(BF16))
        suffix = [jnp.dot(packed[h], neg_upper, preferred_element_type=F32) for h in heads]
        a = []
        for h in heads:
            ah = jnp.exp2(logsig[h] + suffix[h] + state[h][0])
            if mask is not None:
                ah = jnp.where(mask, ah, 0.0)
            a.append(ah.astype(BF16))
        return tuple(
            (state[h][0] - jnp.sum(sp[h], axis=1, keepdims=True),
             state[h][1] + jnp.dot(a[h], v_ref[pl.ds(off, t), hs[h]], preferred_element_type=F32))
            for h in heads)

    init = tuple((jnp.zeros((t, 1), F32), jnp.zeros((t, B_HEAD_DIM), F32)) for _ in range(nh))
    state = tiles(pl.multiple_of(i * t, t), init, causal)

    def body(jj, st):
        return tiles(pl.multiple_of((i - 1 - jj) * t, t), st, None)

    state = lax.fori_loop(0, i, body, state)
    for h in range(nh):
        o_ref[:, hs[h]] = state[h][1].astype(o_ref.dtype)


def _attention(qkv, batch, seq):
    m = qkv.shape[0]
    t = ATTN_T
    nq = seq // t
    w = ATTN_HEADS_PER_STEP * B_HEAD_DIM
    hb = B_WIDTH // w
    return pl.pallas_call(
        _attn_kernel,
        grid=(batch, hb, nq),
        in_specs=[
            pl.BlockSpec((t, w), lambda b, h, i: (b * nq + i, h)),
            pl.BlockSpec((seq, w), lambda b, h, i: (b, hb + h)),
            pl.BlockSpec((seq, w), lambda b, h, i: (b, 2 * hb + h)),
        ],
        out_specs=pl.BlockSpec((t, w), lambda b, h, i: (b * nq + i, h)),
        out_shape=jax.ShapeDtypeStruct((m, B_WIDTH), BF16),
        compiler_params=pltpu.CompilerParams(
            dimension_semantics=("parallel", "parallel", "parallel"), vmem_limit_bytes=VMEM_LIMIT),
        name="attn",
    )(qkv, qkv, qkv)


def _first_max(vals):
    best, idx = vals[0], jnp.zeros(vals[0].shape, jnp.int32)
    for j in range(1, len(vals)):
        better = vals[j] > best
        best = jnp.where(better, vals[j], best)
        idx = jnp.where(better, j, idx)
    return best, idx


def _pick(idx, vals):
    out = vals[0]
    for j in range(1, len(vals)):
        out = jnp.where(idx == j, vals[j], out)
    return out


def _merge_kernel(ob_ref, ya_ref, sga_ref, sgb_ref, x_ref, wob_ref, wo_ref, g_ref, b_ref,
                  rwh_ref, rwl_ref, rb_ref, x1_ref, br_ref, gwt_ref, cnt_ref, base_ref):
    t = x_ref.shape[0]

    @pl.when(pl.program_id(0) == 0)
    def _():
        base_ref[...] = jnp.zeros_like(base_ref)

    yb = jnp.dot(ob_ref[...], wob_ref[...], preferred_element_type=F32)
    mixin = sga_ref[...].astype(F32) * ya_ref[...].astype(F32) + sgb_ref[...].astype(F32) * yb
    mix = jnp.dot(mixin.astype(BF16), wo_ref[...], preferred_element_type=F32)
    x1 = _layer_norm(ALPHA * x_ref[...] + mix, g_ref[...], b_ref[...])
    x1_ref[...] = x1

    x_hi = x1.astype(BF16)
    x_lo = (x1 - x_hi.astype(F32)).astype(BF16)
    logits = (jnp.dot(x_hi, rwh_ref[...], preferred_element_type=F32)
              + jnp.dot(x_lo, rwh_ref[...], preferred_element_type=F32)
              + jnp.dot(x_hi, rwl_ref[...], preferred_element_type=F32))
    lt = jnp.transpose(logits)[:N_EXPERTS, :]
    aff = jax.nn.sigmoid(lt)
    sel = aff + rb_ref[...]
    sel_rows = [sel[e:e + 1, :] for e in range(N_EXPERTS)]
    aff_rows = [aff[e:e + 1, :] for e in range(N_EXPERTS)]
    gscore = []
    for g in range(N_GROUPS):
        s0, s1, s2, s3 = sel_rows[4 * g:4 * g + 4]
        hi1, lo1 = jnp.maximum(s0, s1), jnp.minimum(s0, s1)
        hi2, lo2 = jnp.maximum(s2, s3), jnp.minimum(s2, s3)
        top1 = jnp.maximum(hi1, hi2)
        top2 = jnp.maximum(jnp.minimum(hi1, hi2), jnp.maximum(lo1, lo2))
        gscore.append(top1 + top2)
    _, gidx = _first_max(gscore)
    within = [_pick(gidx, [sel_rows[4 * g + j] for g in range(N_GROUPS)])
              for j in range(EXPERTS_PER_GROUP)]
    awithin = [_pick(gidx, [aff_rows[4 * g + j] for g in range(N_GROUPS)])
               for j in range(EXPERTS_PER_GROUP)]
    _, i0 = _first_max(within)
    masked = [jnp.where(i0 == j, -jnp.inf, within[j]) for j in range(EXPERTS_PER_GROUP)]
    _, i1 = _first_max(masked)
    a0, a1 = _pick(i0, awithin), _pick(i1, awithin)
    denom = a0 + a1
    w0, w1 = a0 / denom, a1 / denom

    swap = i1 < i0
    lo = jnp.where(swap, i1, i0)
    hi = jnp.where(swap, i0, i1)
    w_lo = jnp.where(swap, w1, w0)
    w_hi = jnp.where(swap, w0, w1)
    pair = _pick(lo, PAIR_BASE) + (hi - lo - 1)
    bucket = gidx * PAIRS_PER_GROUP + pair

    biota = lax.broadcasted_iota(jnp.int32, (BUCKET_ROWS, t), 0)
    oh = jnp.where(biota == bucket, 1.0, 0.0)
    row = lax.broadcasted_iota(jnp.int32, (t, t), 0)
    col = lax.broadcasted_iota(jnp.int32, (t, t), 1)
    before = jnp.where(row < col, 1.0, 0.0).astype(BF16)
    tot = base_ref[:, 0:1] + jnp.dot(oh.astype(BF16), before, preferred_element_type=F32)
    rank = jnp.sum(oh * tot, axis=0, keepdims=True)
    slot = lax.broadcasted_iota(jnp.int32, (2, t), 0)
    br_ref[...] = jnp.where(slot == 0, bucket, rank.astype(jnp.int32))
    newbase = base_ref[...] + jnp.sum(oh, axis=1, keepdims=True)
    base_ref[...] = newbase
    cnt_ref[...] = newbase

    wrow = lax.broadcasted_iota(jnp.int32, (LANES, t), 0)
    gw = jnp.where(wrow == 0, w_lo, jnp.where(wrow == 1, w_hi, 0.0))
    gwt_ref[...] = jnp.transpose(gw)


def _merge(ob, ya, gates, xf, wob, wo, layer, g, b, rwh, rwl, rb):
    m = xf.shape[0]
    t = MERGE_T
    return pl.pallas_call(
        _merge_kernel,
        grid=(m // t,),
        in_specs=[
            pl.BlockSpec((t, B_WIDTH), lambda i: (i, 0)),
            pl.BlockSpec((t, D_MODEL), lambda i: (i, 0)),
            pl.BlockSpec((t, D_MODEL), lambda i: (i, 0)),
            pl.BlockSpec((t, D_MODEL), lambda i: (i, 1)),
            pl.BlockSpec((t, D_MODEL), lambda i: (i, 0)),
            _layer_spec((B_WIDTH, D_MODEL), layer),
            _layer_spec((D_MODEL, D_MODEL), layer),
            _const_spec((1, D_MODEL)),
            _const_spec((1, D_MODEL)),
            _const_spec((D_MODEL, LANES)),
            _const_spec((D_MODEL, LANES)),
            _const_spec((N_EXPERTS, 1)),
        ],
        out_specs=[
            pl.BlockSpec((t, D_MODEL), lambda i: (i, 0)),
            pl.BlockSpec((2, t), lambda i: (0, i)),
            pl.BlockSpec((t, LANES), lambda i: (i, 0)),
            pl.BlockSpec((BUCKET_ROWS, LANES), lambda i: (0, 0)),
        ],
        out_shape=[
            jax.ShapeDtypeStruct((m, D_MODEL), F32),
            jax.ShapeDtypeStruct((2, m), jnp.int32),
            jax.ShapeDtypeStruct((m, LANES), F32),
            jax.ShapeDtypeStruct((BUCKET_ROWS, LANES), F32),
        ],
        scratch_shapes=[pltpu.VMEM((BUCKET_ROWS, LANES), F32)],
        compiler_params=pltpu.CompilerParams(
            dimension_semantics=("arbitrary",), vmem_limit_bytes=VMEM_LIMIT),
        name="merge",
    )(ob, ya, gates, gates, xf, wob, wo, g, b, rwh, rwl, rb)


def _dispatch_row_copy(xext_ref, xd_ref, sem, src_row, dst_row):
    return pltpu.make_async_copy(xext_ref.at[pl.ds(src_row, 1), :], xd_ref.at[pl.ds(dst_row, 1), :], sem)


def _zero_fill_copy(zero_ref, xd_ref, sem, dst_row):
    rows = zero_ref.shape[0]
    return pltpu.make_async_copy(zero_ref, xd_ref.at[pl.ds(pl.multiple_of(dst_row, SUBLANES), rows), :], sem)


def _ple_dispatch_kernel(dest_ref, zstart_ref, nv_ref, x1_ref, p_ref, gwt_ref, wpg_ref, wpp_ref,
                         base_ref, xd_ref, xext_ref, zero_ref, sem, zsem):
    t = x1_ref.shape[0]
    first = pl.program_id(0) * t

    @pl.when(pl.program_id(0) == 0)
    def _():
        zero_ref[...] = jnp.zeros_like(zero_ref)
        for k in range(N_BUCKETS):
            _zero_fill_copy(zero_ref, xd_ref, zsem, zstart_ref[k]).start()
        for k in range(N_BUCKETS):
            _zero_fill_copy(zero_ref, xd_ref, zsem, zstart_ref[k]).wait()
        rows = zero_ref.shape[0]
        total_blocks = xd_ref.shape[0] // rows

        def fill(b, c):
            _zero_fill_copy(zero_ref, xd_ref, zsem, b * rows).start()
            return c

        def fill_done(b, c):
            _zero_fill_copy(zero_ref, xd_ref, zsem, b * rows).wait()
            return c

        lax.fori_loop(nv_ref[0], total_blocks, fill, 0)
        lax.fori_loop(nv_ref[0], total_blocks, fill_done, 0)

    x1 = x1_ref[...]
    xext_ref[:, :D_MODEL] = x1
    xext_ref[:, D_MODEL:] = gwt_ref[...]

    for r in range(t):
        _dispatch_row_copy(xext_ref, xd_ref, sem, r, dest_ref[first + r]).start()

    gate = jax.nn.sigmoid(jnp.dot(x1.astype(BF16), wpg_ref[...], preferred_element_type=F32))
    proj = jnp.dot(p_ref[...].astype(BF16), wpp_ref[...], preferred_element_type=F32)
    base_ref[...] = ALPHA * x1 + gate * proj

    for r in range(t):
        _dispatch_row_copy(xext_ref, xd_ref, sem, 0, 0).wait()


def _ple_dispatch(dest, zstart, n_valid, x1, p, layer, gwt, wpg, wpp, xd_rows):
    m, d = x1.shape
    t = DISPATCH_T
    return pl.pallas_call(
        _ple_dispatch_kernel,
        grid_spec=pltpu.PrefetchScalarGridSpec(
            num_scalar_prefetch=3,
            grid=(m // t,),
            in_specs=[
                pl.BlockSpec((t, d), lambda i, *_: (i, 0)),
                pl.BlockSpec((None, t, PLE_DIM), lambda i, *_: (layer, i, 0)),
                pl.BlockSpec((t, LANES), lambda i, *_: (i, 0)),
                _layer_spec((d, d), layer),
                _layer_spec((PLE_DIM, d), layer),
            ],
            out_specs=[
                pl.BlockSpec((t, d), lambda i, *_: (i, 0)),
                pl.BlockSpec(memory_space=pl.ANY),
            ],
            scratch_shapes=[
                pltpu.VMEM((t, XD_WIDTH), F32),
                pltpu.VMEM((EXPERT_TM, XD_WIDTH), F32),
                pltpu.SemaphoreType.DMA(()),
                pltpu.SemaphoreType.DMA(()),
            ],
        ),
        out_shape=[jax.ShapeDtypeStruct((m, d), F32), jax.ShapeDtypeStruct((xd_rows, XD_WIDTH), F32)],
        compiler_params=pltpu.CompilerParams(
            dimension_semantics=("arbitrary",), vmem_limit_bytes=VMEM_LIMIT),
        name="ple_dispatch",
    )(dest, zstart, n_valid, x1, p, gwt, wpg, wpp)


def _ffn(xb, wg_ref, wu_ref, wd_ref):
    g = jnp.dot(xb, wg_ref[...], preferred_element_type=F32)
    u = jnp.dot(xb, wu_ref[...], preferred_element_type=F32)
    h = (jax.nn.silu(g) * u).astype(BF16)
    return jnp.dot(h, wd_ref[...], preferred_element_type=F32)


def _expert_kernel(lo_ref, hi_ref, nv_ref, x_ref, wgl_ref, wul_ref, wdl_ref, wgh_ref, wuh_ref, wdh_ref,
                   y_ref):
    del lo_ref, hi_ref
    b = pl.program_id(0)

    @pl.when(b < nv_ref[0])
    def _():
        xb = x_ref[:, :D_MODEL].astype(BF16)
        w_lo = x_ref[:, D_MODEL:D_MODEL + 1]
        w_hi = x_ref[:, D_MODEL + 1:D_MODEL + 2]
        y_ref[...] = (w_lo * _ffn(xb, wgl_ref, wul_ref, wdl_ref)
                      + w_hi * _ffn(xb, wgh_ref, wuh_ref, wdh_ref))

    @pl.when(b >= nv_ref[0])
    def _():
        y_ref[...] = jnp.zeros_like(y_ref)


def _experts(blk_lo, blk_hi, n_valid, xd, wg, wu, wd, layer, n_blocks):
    d = D_MODEL
    tm = EXPERT_TM

    def x_map(b, lo, hi, nv):
        return (jnp.minimum(b, jnp.maximum(nv[0] - 1, 0)), 0)

    def lo_map(b, lo, hi, nv):
        return (layer, lo[b], 0, 0)

    def hi_map(b, lo, hi, nv):
        return (layer, hi[b], 0, 0)

    return pl.pallas_call(
        _expert_kernel,
        grid_spec=pltpu.PrefetchScalarGridSpec(
            num_scalar_prefetch=3,
            grid=(n_blocks,),
            in_specs=[
                pl.BlockSpec((tm, XD_WIDTH), x_map),
                pl.BlockSpec((None, None, d, D_EXPERT), lo_map),
                pl.BlockSpec((None, None, d, D_EXPERT), lo_map),
                pl.BlockSpec((None, None, D_EXPERT, d), lo_map),
                pl.BlockSpec((None, None, d, D_EXPERT), hi_map),
                pl.BlockSpec((None, None, d, D_EXPERT), hi_map),
                pl.BlockSpec((None, None, D_EXPERT, d), hi_map),
            ],
            out_specs=pl.BlockSpec((tm, d), lambda b, lo, hi, nv: (b, 0)),
        ),
        out_shape=jax.ShapeDtypeStruct((n_blocks * tm, d), F32),
        compiler_params=pltpu.CompilerParams(
            dimension_semantics=("arbitrary",), vmem_limit_bytes=EXPERT_VMEM_LIMIT),
        name="experts",
    )(blk_lo, blk_hi, n_valid, xd, wg, wu, wd, wg, wu, wd)


def _combine_row_copy(y_ref, buf_ref, sems, slot, r, src_row):
    return pltpu.make_async_copy(y_ref.at[pl.ds(src_row, 1), :], buf_ref.at[slot, pl.ds(r, 1), :],
                                 sems.at[slot])


def _final_kernel(dest_ref, base_ref, g_ref, b_ref, y_ref, xo_ref, xob_ref, buf_ref, sems):
    t = base_ref.shape[0]
    i = pl.program_id(0)

    def issue(tile, slot):
        for r in range(t):
            _combine_row_copy(y_ref, buf_ref, sems, slot, r, dest_ref[tile * t + r]).start()

    @pl.when(i == 0)
    def _():
        issue(0, 0)

    nxt = i + 1
    for s in range(2):
        @pl.when((nxt < pl.num_programs(0)) & (nxt % 2 == s))
        def _():
            issue(nxt, s)

    for s in range(2):
        @pl.when(i % 2 == s)
        def _():
            for r in range(t):
                _combine_row_copy(y_ref, buf_ref, sems, s, 0, 0).wait()

    x2 = _layer_norm(base_ref[...] + buf_ref[i % 2], g_ref[...], b_ref[...])
    xo_ref[...] = x2
    xob_ref[...] = x2.astype(BF16)


def _final(dest, base, g, b, y):
    m, d = base.shape
    t = FINAL_T
    return pl.pallas_call(
        _final_kernel,
        grid_spec=pltpu.PrefetchScalarGridSpec(
            num_scalar_prefetch=1,
            grid=(m // t,),
            in_specs=[
                pl.BlockSpec((t, d), lambda i, dest: (i, 0)),
                pl.BlockSpec((1, d), lambda i, dest: (0, 0), pipeline_mode=pl.Buffered(1)),
                pl.BlockSpec((1, d), lambda i, dest: (0, 0), pipeline_mode=pl.Buffered(1)),
                pl.BlockSpec(memory_space=pl.ANY),
            ],
            out_specs=[
                pl.BlockSpec((t, d), lambda i, dest: (i, 0)),
                pl.BlockSpec((t, d), lambda i, dest: (i, 0)),
            ],
            scratch_shapes=[pltpu.VMEM((2, t, d), F32), pltpu.SemaphoreType.DMA((2,))],
        ),
        out_shape=[jax.ShapeDtypeStruct((m, d), F32), jax.ShapeDtypeStruct((m, d), BF16)],
        compiler_params=pltpu.CompilerParams(
            dimension_semantics=("arbitrary",), vmem_limit_bytes=VMEM_LIMIT),
        name="final",
    )(dest, base, g, b, y)


def kernel(x, p, w_in, gmlp_ln_g, gmlp_ln_b, gmlp_ws, gmlp_bs, w_out_a, w_out_b, w_o, ln1_g, ln1_b,
           router_w, router_bias, exp_w_gate, exp_w_up, exp_w_down, ple_w_gate, ple_w_proj,
           ln2_g, ln2_b):
    batch, seq, d = x.shape
    m = batch * seq
    tm = EXPERT_TM
    n_blocks = m // tm + N_BUCKETS
    xd_rows = (n_blocks + 1) * tm

    xf = x.reshape(m, d)
    xb = xf.astype(BF16)
    p_flat = p.reshape(DEPTH, m, PLE_DIM)
    rw = jnp.pad(router_w.astype(F32), ((0, 0), (0, LANES - N_EXPERTS)))
    rwh = rw.astype(BF16)
    rwl = (rw - rwh.astype(F32)).astype(BF16)
    rb = router_bias.astype(F32).reshape(N_EXPERTS, 1)
    bucket_ids = jnp.arange(N_BUCKETS, dtype=jnp.int32)
    block_ids = jnp.arange(n_blocks, dtype=jnp.int32)
    w_out_a_b, w_out_b_b, w_o_b = w_out_a.astype(BF16), w_out_b.astype(BF16), w_o.astype(BF16)
    wpg_b, wpp_b = ple_w_gate.astype(BF16), ple_w_proj.astype(BF16)
    wg_b, wu_b, wd_b = exp_w_gate.astype(BF16), exp_w_up.astype(BF16), exp_w_down.astype(BF16)

    for i in range(DEPTH):
        ug = _inproj(xb, w_in, i, 0, 2 * A_WIDTH, jax.nn.gelu)
        qkv = _inproj(xb, w_in, i, 2 * A_WIDTH, 3 * B_WIDTH, lambda a: a,
                      first_tile_scale=B_HEAD_DIM ** -0.5 * LOG2E)
        gates = _inproj(xb, w_in, i, 2 * A_WIDTH + 3 * B_WIDTH, 2 * D_MODEL, jax.nn.sigmoid)

        bias = jnp.repeat(jnp.transpose(gmlp_bs[i]), A_HEAD_DIM, axis=1)
        ya = _gmlp(ug, gmlp_ln_g[i].reshape(1, A_WIDTH), gmlp_ln_b[i].reshape(1, A_WIDTH),
                   gmlp_ws, i, bias, w_out_a_b)
        ob = _attention(qkv, batch, seq)

        x1, br, gwt, cnt = _merge(
            ob, ya, gates, xf, w_out_b_b, w_o_b, i,
            ln1_g[i].reshape(1, d), ln1_b[i].reshape(1, d), rwh, rwl, rb)

        counts = cnt[:N_BUCKETS, 0].astype(jnp.int32)
        nblk = jnp.maximum((counts + tm - 1) // tm, 1)
        blk_end = jnp.cumsum(nblk)
        pstart = (blk_end - nblk) * tm
        dest = jnp.sum(jnp.where(br[0][None, :] == bucket_ids[:, None], pstart[:, None], 0), axis=0) + br[1]
        zstart = pstart + counts // SUBLANES * SUBLANES
        blk_bucket = jnp.minimum(
            jnp.sum((block_ids[:, None] >= blk_end[None, :]).astype(jnp.int32), axis=1), N_BUCKETS - 1)
        blk_lo = jnp.asarray(BUCKET_LO)[blk_bucket]
        blk_hi = jnp.asarray(BUCKET_HI)[blk_bucket]
        n_valid = blk_end[-1:].astype(jnp.int32)

        base, xd = _ple_dispatch(dest, zstart, n_valid, x1, p_flat, i, gwt, wpg_b, wpp_b, xd_rows)
        y = _experts(blk_lo, blk_hi, n_valid, xd, wg_b, wu_b, wd_b, i, n_blocks)
        xf, xb = _final(dest, base, ln2_g[i].reshape(1, d), ln2_b[i].reshape(1, d), y)

    return xf.reshape(batch, seq, d)
```

```python
import functools

import jax
import jax.numpy as jnp
import numpy as np
from jax import lax
from jax.experimental import pallas as pl
from jax.experimental.pallas import tpu as pltpu

D_MODEL = 2048
DEPTH = 4
A_HEADS = 8
A_HEAD_DIM = 128
A_WIDTH = A_HEADS * A_HEAD_DIM
CHUNK = 128
B_HEADS = 8
B_HEAD_DIM = 128
B_WIDTH = B_HEADS * B_HEAD_DIM
N_EXPERTS = 16
N_GROUPS = 4
EXPERTS_PER_GROUP = N_EXPERTS // N_GROUPS
D_EXPERT = 1024
PLE_DIM = 256
ALPHA = (2 * DEPTH) ** 0.25
LN_EPS = 1e-5
LOG2E = 1.4426950408889634

LANES = 128
SUBLANES = 8
VMEM_LIMIT = 56 * 1024 * 1024
EXPERT_VMEM_LIMIT = 62 * 1024 * 1024

F32 = jnp.float32
BF16 = jnp.bfloat16

PAIRS = [(lo, hi) for lo in range(EXPERTS_PER_GROUP) for hi in range(lo + 1, EXPERTS_PER_GROUP)]
PAIRS_PER_GROUP = len(PAIRS)
N_BUCKETS = N_GROUPS * PAIRS_PER_GROUP
BUCKET_ROWS = 32
BUCKET_LO = np.array([g * EXPERTS_PER_GROUP + lo for g in range(N_GROUPS) for lo, _ in PAIRS], np.int32)
BUCKET_HI = np.array([g * EXPERTS_PER_GROUP + hi for g in range(N_GROUPS) for _, hi in PAIRS], np.int32)
PAIR_BASE = [PAIRS.index((lo, lo + 1)) for lo in range(EXPERTS_PER_GROUP - 1)]
XD_WIDTH = D_MODEL + LANES

INPROJ_TM = 1024
INPROJ_TN = 1024
INPROJ_CHUNKS = 4
GMLP_T = 512
ATTN_T = 256
ATTN_HEADS_PER_STEP = 8
MERGE_T = 256
DISPATCH_T = 256
EXPERT_TM = 128
FINAL_T = 256


def _layer_norm(xf, gain, bias):
    mu = jnp.mean(xf, axis=-1, keepdims=True)
    xc = xf - mu
    var = jnp.mean(xc * xc, axis=-1, keepdims=True)
    return xc * lax.rsqrt(var + LN_EPS) * gain + bias


def _const_spec(shape):
    nd = len(shape)
    return pl.BlockSpec(shape, lambda *_: (0,) * nd, pipeline_mode=pl.Buffered(1))


def _layer_spec(shape, layer):
    nd = len(shape)
    return pl.BlockSpec((None,) + tuple(shape), lambda *_: (layer,) + (0,) * nd,
                        pipeline_mode=pl.Buffered(1))


def _inproj_kernel(x_ref, w_ref, *refs, act, first_tile_scale, n_cast):
    cast_src, o_ref, cast_dst, wb_ref = refs[:n_cast], refs[n_cast], refs[n_cast + 1:-1], refs[-1]

    @pl.when(pl.program_id(1) == 0)
    def _():
        wb_ref[...] = w_ref[...].astype(BF16)

    for src, dst in zip(cast_src, cast_dst):
        dst[...] = src[...].astype(BF16)

    rows = x_ref.shape[0] // INPROJ_CHUNKS
    for c in range(INPROJ_CHUNKS):
        rs = slice(c * rows, (c + 1) * rows)
        y = act(jnp.dot(x_ref[rs, :], wb_ref[...], preferred_element_type=F32))
        if first_tile_scale is not None:
            y = y * jnp.where(pl.program_id(0) == 0, first_tile_scale, 1.0)
        o_ref[rs, :] = y.astype(o_ref.dtype)


def _inproj(xb, w, layer, col0, ncols, act, first_tile_scale=None, cast=()):
    m, k = xb.shape
    tm, tn = INPROJ_TM, INPROJ_TN
    cb0 = col0 // tn
    n_i = m // tm
    steps = (ncols // tn) * n_i
    cast_in_specs, cast_out_specs, cast_shapes = [], [], []
    for arr in cast:
        _, rows, cols = arr.shape
        slab = rows // steps
        cast_in_specs.append(pl.BlockSpec((None, slab, cols), lambda j, i: (layer, j * n_i + i, 0)))
        cast_out_specs.append(pl.BlockSpec((slab, cols), lambda j, i: (j * n_i + i, 0)))
        cast_shapes.append(jax.ShapeDtypeStruct((rows, cols), BF16))
    out = pl.pallas_call(
        functools.partial(_inproj_kernel, act=act, first_tile_scale=first_tile_scale, n_cast=len(cast)),
        grid=(ncols // tn, n_i),
        in_specs=[
            pl.BlockSpec((tm, k), lambda j, i: (i, 0)),
            pl.BlockSpec((None, k, tn), lambda j, i: (layer, 0, cb0 + j), pipeline_mode=pl.Buffered(1)),
        ] + cast_in_specs,
        out_specs=[pl.BlockSpec((tm, tn), lambda j, i: (i, j))] + cast_out_specs,
        out_shape=[jax.ShapeDtypeStruct((m, ncols), BF16)] + cast_shapes,
        scratch_shapes=[pltpu.VMEM((k, tn), BF16)],
        compiler_params=pltpu.CompilerParams(
            dimension_semantics=("arbitrary", "arbitrary"), vmem_limit_bytes=VMEM_LIMIT),
        name="inproj",
    )(xb, w, *cast)
    return out if cast else out[0]


def _gmlp_kernel(u_ref, gv_ref, lng_ref, lnb_ref, ws_ref, bias_ref, wo_ref, o_ref, yin_ref):
    t = u_ref.shape[0]
    nc = t // CHUNK
    v = _layer_norm(gv_ref[...].astype(F32), lng_ref[...], lnb_ref[...]).astype(BF16)
    row = lax.broadcasted_iota(jnp.int32, (CHUNK, CHUNK), 0)
    col = lax.broadcasted_iota(jnp.int32, (CHUNK, CHUNK), 1)
    tril = row >= col
    for h in range(A_HEADS):
        hs = slice(h * A_HEAD_DIM, (h + 1) * A_HEAD_DIM)
        wsm = jnp.where(tril, ws_ref[h], 0.0).astype(BF16)
        vh = jnp.concatenate([v[c * CHUNK:(c + 1) * CHUNK, hs] for c in range(nc)], axis=1)
        f = jnp.dot(wsm, vh, preferred_element_type=F32)
        bh = bias_ref[:, hs]
        for c in range(nc):
            rs = slice(c * CHUNK, (c + 1) * CHUNK)
            fc = f[:, c * A_HEAD_DIM:(c + 1) * A_HEAD_DIM] + bh
            yin_ref[rs, hs] = (u_ref[rs, hs].astype(F32) * fc).astype(BF16)
    o_ref[...] = jnp.dot(yin_ref[...], wo_ref[...], preferred_element_type=F32).astype(o_ref.dtype)


def _gmlp(ug, lng, lnb, ws, layer, bias, wo):
    m = ug.shape[0]
    t = GMLP_T
    return pl.pallas_call(
        _gmlp_kernel,
        grid=(m // t,),
        in_specs=[
            pl.BlockSpec((t, A_WIDTH), lambda i: (i, 0)),
            pl.BlockSpec((t, A_WIDTH), lambda i: (i, 1)),
            _const_spec((1, A_WIDTH)),
            _const_spec((1, A_WIDTH)),
            _layer_spec((A_HEADS, CHUNK, CHUNK), layer),
            _const_spec((CHUNK, A_WIDTH)),
            _layer_spec((A_WIDTH, D_MODEL), layer),
        ],
        out_specs=pl.BlockSpec((t, D_MODEL), lambda i: (i, 0)),
        out_shape=jax.ShapeDtypeStruct((m, D_MODEL), BF16),
        scratch_shapes=[pltpu.VMEM((t, A_WIDTH), BF16)],
        compiler_params=pltpu.CompilerParams(
            dimension_semantics=("parallel",), vmem_limit_bytes=VMEM_LIMIT),
        name="gmlp",
    )(ug, ug, lng, lnb, ws, bias, wo)


def _attn_kernel(q_ref, k_ref, v_ref, o_ref):
    t = q_ref.shape[0]
    nh = q_ref.shape[1] // B_HEAD_DIM
    heads = range(nh)
    i = pl.program_id(2)
    row = lax.broadcasted_iota(jnp.int32, (t, t), 0)
    col = lax.broadcasted_iota(jnp.int32, (t, t), 1)
    neg_upper = jnp.where(row > col, -1.0, 0.0).astype(BF16)
    causal = col < row
    hs = [slice(h * B_HEAD_DIM, (h + 1) * B_HEAD_DIM) for h in heads]
    qs = [q_ref[:, s] for s in hs]

    def tiles(off, state, mask):
        z = [lax.dot_general(qs[h], k_ref[pl.ds(off, t), hs[h]], (((1,), (1,)), ((), ())),
                             preferred_element_type=F32) for h in heads]
        sp, logsig, packed = [], [], []
        for h in heads:
            neg_abs = pltpu.bitcast(pltpu.bitcast(z[h], jnp.uint32) | jnp.uint32(0x80000000), F32)
            s = jnp.maximum(z[h], 0.0) + jnp.log(1.0 + jnp.exp2(neg_abs)) * LOG2E
            logsig.append(z[h] - s)
            if mask is not None:
                s = jnp.where(mask, s, 0.0)
            sp.append(s)
            packed.append(s.astype(BF16))
        suffix = [jnp.dot(packed[h], neg_upper, preferred_element_type=F32) for h in heads]
        a = []
        for h in heads:
            ah = jnp.exp2(logsig[h] + suffix[h] + state[h][0])
            if mask is not None:
                ah = jnp.where(mask, ah, 0.0)
            a.append(ah.astype(BF16))
        return tuple(
            (state[h][0] - jnp.sum(sp[h], axis=1, keepdims=True),
             state[h][1] + jnp.dot(a[h], v_ref[pl.ds(off, t), hs[h]], preferred_element_type=F32))
            for h in heads)

    init = tuple((jnp.zeros((t, 1), F32), jnp.zeros((t, B_HEAD_DIM), F32)) for _ in range(nh))
    state = tiles(pl.multiple_of(i * t, t), init, causal)

    def body(jj, st):
        return tiles(pl.multiple_of((i - 1 - jj) * t, t), st, None)

    state = lax.fori_loop(0, i, body, state)
    for h in range(nh):
        o_ref[:, hs[h]] = state[h][1].astype(o_ref.dtype)


def _attention(qkv, batch, seq):
    m = qkv.shape[0]
    t = ATTN_T
    nq = seq // t
    w = ATTN_HEADS_PER_STEP * B_HEAD_DIM
    hb = B_WIDTH // w
    return pl.pallas_call(
        _attn_kernel,
        grid=(batch, hb, nq),
        in_specs=[
            pl.BlockSpec((t, w), lambda b, h, i: (b * nq + i, h)),
            pl.BlockSpec((seq, w), lambda b, h, i: (b, hb + h)),
            pl.BlockSpec((seq, w), lambda b, h, i: (b, 2 * hb + h)),
        ],
        out_specs=pl.BlockSpec((t, w), lambda b, h, i: (b * nq + i, h)),
        out_shape=jax.ShapeDtypeStruct((m, B_WIDTH), BF16),
        compiler_params=pltpu.CompilerParams(
            dimension_semantics=("parallel", "parallel", "parallel"), vmem_limit_bytes=VMEM_LIMIT),
        name="attn",
    )(qkv, qkv, qkv)


def _first_max(vals):
    best, idx = vals[0], jnp.zeros(vals[0].shape, jnp.int32)
    for j in range(1, len(vals)):
        better = vals[j] > best
        best = jnp.where(better, vals[j], best)
        idx = jnp.where(better, j, idx)
    return best, idx


def _pick(idx, vals):
    out = vals[0]
    for j in range(1, len(vals)):
        out = jnp.where(idx == j, vals[j], out)
    return out


def _merge_kernel(ob_ref, ya_ref, sga_ref, sgb_ref, x_ref, wob_ref, wo_ref, g_ref, b_ref,
                  rw2_ref, rb_ref, x1_ref, br_ref, gwt_ref, cnt_ref, base_ref, logit_ref):
    t = x_ref.shape[0]
    step = pl.program_id(0)

    @pl.when(step == 0)
    def _():
        base_ref[...] = jnp.zeros_like(base_ref)
        logit_ref[...] = jnp.zeros_like(logit_ref)

    live = jnp.where(step > 0, 1.0, 0.0)

    lt = jnp.transpose(logit_ref[...])[:N_EXPERTS, :]
    aff = jax.nn.sigmoid(lt)
    sel = aff + rb_ref[...]
    sel_rows = [sel[e:e + 1, :] for e in range(N_EXPERTS)]
    aff_rows = [aff[e:e + 1, :] for e in range(N_EXPERTS)]
    gscore = []
    for g in range(N_GROUPS):
        s0, s1, s2, s3 = sel_rows[4 * g:4 * g + 4]
        hi1, lo1 = jnp.maximum(s0, s1), jnp.minimum(s0, s1)
        hi2, lo2 = jnp.maximum(s2, s3), jnp.minimum(s2, s3)
        top1 = jnp.maximum(hi1, hi2)
        top2 = jnp.maximum(jnp.minimum(hi1, hi2), jnp.maximum(lo1, lo2))
        gscore.append(top1 + top2)
    _, gidx = _first_max(gscore)
    within = [_pick(gidx, [sel_rows[4 * g + j] for g in range(N_GROUPS)])
              for j in range(EXPERTS_PER_GROUP)]
    awithin = [_pick(gidx, [aff_rows[4 * g + j] for g in range(N_GROUPS)])
               for j in range(EXPERTS_PER_GROUP)]
    _, i0 = _first_max(within)
    masked = [jnp.where(i0 == j, -jnp.inf, within[j]) for j in range(EXPERTS_PER_GROUP)]
    _, i1 = _first_max(masked)
    a0, a1 = _pick(i0, awithin), _pick(i1, awithin)
    denom = a0 + a1
    w0, w1 = a0 / denom, a1 / denom

    swap = i1 < i0
    lo = jnp.where(swap, i1, i0)
    hi = jnp.where(swap, i0, i1)
    w_lo = jnp.where(swap, w1, w0)
    w_hi = jnp.where(swap, w0, w1)
    pair = _pick(lo, PAIR_BASE) + (hi - lo - 1)
    bucket = gidx * PAIRS_PER_GROUP + pair

    biota = lax.broadcasted_iota(jnp.int32, (BUCKET_ROWS, t), 0)
    oh = jnp.where(biota == bucket, 1.0, 0.0)
    row = lax.broadcasted_iota(jnp.int32, (t, t), 0)
    col = lax.broadcasted_iota(jnp.int32, (t, t), 1)
    before = jnp.where(row < col, 1.0, 0.0).astype(BF16)
    tot = base_ref[:, 0:1] + jnp.dot(oh.astype(BF16), before, preferred_element_type=F32)
    rank = jnp.sum(oh * tot, axis=0, keepdims=True)
    slot = lax.broadcasted_iota(jnp.int32, (2, t), 0)
    br_ref[...] = jnp.where(slot == 0, bucket, rank.astype(jnp.int32))
    newbase = base_ref[...] + live * jnp.sum(oh, axis=1, keepdims=True)
    base_ref[...] = newbase
    cnt_ref[...] = newbase

    wrow = lax.broadcasted_iota(jnp.int32, (LANES, t), 0)
    gw = jnp.where(wrow == 0, w_lo, jnp.where(wrow == 1, w_hi, 0.0))
    gwt_ref[...] = jnp.transpose(gw)

    yb = jnp.dot(ob_ref[...], wob_ref[...], preferred_element_type=F32)
    mixin = sga_ref[...].astype(F32) * ya_ref[...].astype(F32) + sgb_ref[...].astype(F32) * yb
    mix = jnp.dot(mixin.astype(BF16), wo_ref[...], preferred_element_type=F32)
    x1 = _layer_norm(ALPHA * x_ref[...] + mix, g_ref[...], b_ref[...])
    x1_ref[...] = x1
    x_hi = x1.astype(BF16)
    x_lo = (x1 - x_hi.astype(F32)).astype(BF16)
    r_hi = jnp.dot(x_hi, rw2_ref[...], preferred_element_type=F32)
    r_lo = jnp.dot(x_lo, rw2_ref[...], preferred_element_type=F32)
    logit_ref[...] = ((r_hi[:, :LANES] + r_hi[:, LANES:])
                      + (r_lo[:, :LANES] + r_lo[:, LANES:]))


def _merge(ob, ya, gates, xf, wob, wo, layer, g, b, rw2, rb):
    m = xf.shape[0]
    t = MERGE_T
    last = m // t - 1

    def cur(col):
        return lambda i: (jnp.minimum(i, last), col)

    return pl.pallas_call(
        _merge_kernel,
        grid=(m // t + 1,),
        in_specs=[
            pl.BlockSpec((t, B_WIDTH), cur(0)),
            pl.BlockSpec((t, D_MODEL), cur(0)),
            pl.BlockSpec((t, D_MODEL), cur(0)),
            pl.BlockSpec((t, D_MODEL), cur(1)),
            pl.BlockSpec((t, D_MODEL), cur(0)),
            _layer_spec((B_WIDTH, D_MODEL), layer),
            _layer_spec((D_MODEL, D_MODEL), layer),
            _const_spec((1, D_MODEL)),
            _const_spec((1, D_MODEL)),
            _const_spec((D_MODEL, 2 * LANES)),
            _const_spec((N_EXPERTS, 1)),
        ],
        out_specs=[
            pl.BlockSpec((t, D_MODEL), cur(0)),
            pl.BlockSpec((2, t), lambda i: (0, jnp.maximum(i - 1, 0))),
            pl.BlockSpec((t, LANES), lambda i: (jnp.maximum(i - 1, 0), 0)),
            pl.BlockSpec((BUCKET_ROWS, LANES), lambda i: (0, 0)),
        ],
        out_shape=[
            jax.ShapeDtypeStruct((m, D_MODEL), F32),
            jax.ShapeDtypeStruct((2, m), jnp.int32),
            jax.ShapeDtypeStruct((m, LANES), F32),
            jax.ShapeDtypeStruct((BUCKET_ROWS, LANES), F32),
        ],
        scratch_shapes=[pltpu.VMEM((BUCKET_ROWS, LANES), F32), pltpu.VMEM((t, LANES), F32)],
        compiler_params=pltpu.CompilerParams(
            dimension_semantics=("arbitrary",), vmem_limit_bytes=VMEM_LIMIT),
        name="merge",
    )(ob, ya, gates, gates, xf, wob, wo, g, b, rw2, rb)


def _dispatch_row_copy(xext_ref, xd_ref, sem, src_row, dst_row):
    return pltpu.make_async_copy(xext_ref.at[pl.ds(src_row, 1), :], xd_ref.at[pl.ds(dst_row, 1), :], sem)


def _zero_fill_copy(zero_ref, xd_ref, sem, dst_row):
    rows = zero_ref.shape[0]
    return pltpu.make_async_copy(zero_ref, xd_ref.at[pl.ds(pl.multiple_of(dst_row, SUBLANES), rows), :], sem)


def _ple_dispatch_kernel(dest_ref, zstart_ref, nv_ref, x1_ref, p_ref, gwt_ref, wpg_ref, wpp_ref,
                         base_ref, xd_ref, xext_ref, zero_ref, sem, zsem):
    t = x1_ref.shape[0]
    first = pl.program_id(0) * t

    @pl.when(pl.program_id(0) == 0)
    def _():
        zero_ref[...] = jnp.zeros_like(zero_ref)
        for k in range(N_BUCKETS):
            _zero_fill_copy(zero_ref, xd_ref, zsem, zstart_ref[k]).start()
        for k in range(N_BUCKETS):
            _zero_fill_copy(zero_ref, xd_ref, zsem, zstart_ref[k]).wait()
        rows = zero_ref.shape[0]
        total_blocks = xd_ref.shape[0] // rows

        def fill(b, c):
            _zero_fill_copy(zero_ref, xd_ref, zsem, b * rows).start()
            return c

        def fill_done(b, c):
            _zero_fill_copy(zero_ref, xd_ref, zsem, b * rows).wait()
            return c

        lax.fori_loop(nv_ref[0], total_blocks, fill, 0)
        lax.fori_loop(nv_ref[0], total_blocks, fill_done, 0)

    x1 = x1_ref[...]
    xext_ref[:, :D_MODEL] = x1
    xext_ref[:, D_MODEL:] = gwt_ref[...]

    for r in range(t):
        _dispatch_row_copy(xext_ref, xd_ref, sem, r, dest_ref[first + r]).start()

    gate = jax.nn.sigmoid(jnp.dot(x1.astype(BF16), wpg_ref[...], preferred_element_type=F32))
    proj = jnp.dot(p_ref[...].astype(BF16), wpp_ref[...], preferred_element_type=F32)
    base_ref[...] = ALPHA * x1 + gate * proj

    for r in range(t):
        _dispatch_row_copy(xext_ref, xd_ref, sem, 0, 0).wait()


def _ple_dispatch(dest, zstart, n_valid, x1, p, layer, gwt, wpg, wpp, xd_rows):
    m, d = x1.shape
    t = DISPATCH_T
    return pl.pallas_call(
        _ple_dispatch_kernel,
        grid_spec=pltpu.PrefetchScalarGridSpec(
            num_scalar_prefetch=3,
            grid=(m // t,),
            in_specs=[
                pl.BlockSpec((t, d), lambda i, *_: (i, 0)),
                pl.BlockSpec((None, t, PLE_DIM), lambda i, *_: (layer, i, 0)),
                pl.BlockSpec((t, LANES), lambda i, *_: (i, 0)),
                _layer_spec((d, d), layer),
                _layer_spec((PLE_DIM, d), layer),
            ],
            out_specs=[
                pl.BlockSpec((t, d), lambda i, *_: (i, 0)),
                pl.BlockSpec(memory_space=pl.ANY),
            ],
            scratch_shapes=[
                pltpu.VMEM((t, XD_WIDTH), F32),
                pltpu.VMEM((EXPERT_TM, XD_WIDTH), F32),
                pltpu.SemaphoreType.DMA(()),
                pltpu.SemaphoreType.DMA(()),
            ],
        ),
        out_shape=[jax.ShapeDtypeStruct((m, d), F32), jax.ShapeDtypeStruct((xd_rows, XD_WIDTH), F32)],
        compiler_params=pltpu.CompilerParams(
            dimension_semantics=("arbitrary",), vmem_limit_bytes=VMEM_LIMIT),
        name="ple_dispatch",
    )(dest, zstart, n_valid, x1, p, gwt, wpg, wpp)


def _ffn(xb, wg_ref, wu_ref, wd_ref):
    g = jnp.dot(xb, wg_ref[...], preferred_element_type=F32)
    u = jnp.dot(xb, wu_ref[...], preferred_element_type=F32)
    h = (jax.nn.silu(g) * u).astype(BF16)
    return jnp.dot(h, wd_ref[...], preferred_element_type=F32)


def _expert_kernel(lo_ref, hi_ref, nv_ref, x_ref, wgl_ref, wul_ref, wdl_ref, wgh_ref, wuh_ref, wdh_ref,
                   y_ref):
    del lo_ref, hi_ref
    b = pl.program_id(0)

    @pl.when(b < nv_ref[0])
    def _():
        xb = x_ref[:, :D_MODEL].astype(BF16)
        w_lo = x_ref[:, D_MODEL:D_MODEL + 1]
        w_hi = x_ref[:, D_MODEL + 1:D_MODEL + 2]
        y_ref[...] = (w_lo * _ffn(xb, wgl_ref, wul_ref, wdl_ref)
                      + w_hi * _ffn(xb, wgh_ref, wuh_ref, wdh_ref))

    @pl.when(b >= nv_ref[0])
    def _():
        y_ref[...] = jnp.zeros_like(y_ref)


def _experts(blk_lo, blk_hi, n_valid, xd, wg, wu, wd, n_blocks):
    d = D_MODEL
    tm = EXPERT_TM

    def x_map(b, lo, hi, nv):
        return (jnp.minimum(b, jnp.maximum(nv[0] - 1, 0)), 0)

    def lo_map(b, lo, hi, nv):
        return (lo[b], 0, 0)

    def hi_map(b, lo, hi, nv):
        return (hi[b], 0, 0)

    return pl.pallas_call(
        _expert_kernel,
        grid_spec=pltpu.PrefetchScalarGridSpec(
            num_scalar_prefetch=3,
            grid=(n_blocks,),
            in_specs=[
                pl.BlockSpec((tm, XD_WIDTH), x_map),
                pl.BlockSpec((None, d, D_EXPERT), lo_map),
                pl.BlockSpec((None, d, D_EXPERT), lo_map),
                pl.BlockSpec((None, D_EXPERT, d), lo_map),
                pl.BlockSpec((None, d, D_EXPERT), hi_map),
                pl.BlockSpec((None, d, D_EXPERT), hi_map),
                pl.BlockSpec((None, D_EXPERT, d), hi_map),
            ],
            out_specs=pl.BlockSpec((tm, d), lambda b, lo, hi, nv: (b, 0)),
        ),
        out_shape=jax.ShapeDtypeStruct((n_blocks * tm, d), F32),
        compiler_params=pltpu.CompilerParams(
            dimension_semantics=("arbitrary",), vmem_limit_bytes=EXPERT_VMEM_LIMIT),
        name="experts",
    )(blk_lo, blk_hi, n_valid, xd, wg, wu, wd, wg, wu, wd)


def _combine_row_copy(y_ref, buf_ref, sems, slot, r, src_row):
    return pltpu.make_async_copy(y_ref.at[pl.ds(src_row, 1), :], buf_ref.at[slot, pl.ds(r, 1), :],
                                 sems.at[slot])


def _final_kernel(dest_ref, base_ref, g_ref, b_ref, y_ref, xo_ref, xob_ref, buf_ref, sems):
    t = base_ref.shape[0]
    i = pl.program_id(0)

    def issue(tile, slot):
        for r in range(t):
            _combine_row_copy(y_ref, buf_ref, sems, slot, r, dest_ref[tile * t + r]).start()

    @pl.when(i == 0)
    def _():
        issue(0, 0)

    nxt = i + 1
    for s in range(2):
        @pl.when((nxt < pl.num_programs(0)) & (nxt % 2 == s))
        def _():
            issue(nxt, s)

    for s in range(2):
        @pl.when(i % 2 == s)
        def _():
            for r in range(t):
                _combine_row_copy(y_ref, buf_ref, sems, s, 0, 0).wait()

    x2 = _layer_norm(base_ref[...] + buf_ref[i % 2], g_ref[...], b_ref[...])
    xo_ref[...] = x2
    xob_ref[...] = x2.astype(BF16)


def _final(dest, base, g, b, y):
    m, d = base.shape
    t = FINAL_T
    return pl.pallas_call(
        _final_kernel,
        grid_spec=pltpu.PrefetchScalarGridSpec(
            num_scalar_prefetch=1,
            grid=(m // t,),
            in_specs=[
                pl.BlockSpec((t, d), lambda i, dest: (i, 0)),
                pl.BlockSpec((1, d), lambda i, dest: (0, 0), pipeline_mode=pl.Buffered(1)),
                pl.BlockSpec((1, d), lambda i, dest: (0, 0), pipeline_mode=pl.Buffered(1)),
                pl.BlockSpec(memory_space=pl.ANY),
            ],
            out_specs=[
                pl.BlockSpec((t, d), lambda i, dest: (i, 0)),
                pl.BlockSpec((t, d), lambda i, dest: (i, 0)),
            ],
            scratch_shapes=[pltpu.VMEM((2, t, d), F32), pltpu.SemaphoreType.DMA((2,))],
        ),
        out_shape=[jax.ShapeDtypeStruct((m, d), F32), jax.ShapeDtypeStruct((m, d), BF16)],
        compiler_params=pltpu.CompilerParams(
            dimension_semantics=("arbitrary",), vmem_limit_bytes=VMEM_LIMIT),
        name="final",
    )(dest, base, g, b, y)


def kernel(x, p, w_in, gmlp_ln_g, gmlp_ln_b, gmlp_ws, gmlp_bs, w_out_a, w_out_b, w_o, ln1_g, ln1_b,
           router_w, router_bias, exp_w_gate, exp_w_up, exp_w_down, ple_w_gate, ple_w_proj,
           ln2_g, ln2_b):
    batch, seq, d = x.shape
    m = batch * seq
    tm = EXPERT_TM
    n_blocks = m // tm + N_BUCKETS
    xd_rows = (n_blocks + 1) * tm

    xf = x.reshape(m, d)
    xb = xf.astype(BF16)
    p_flat = p.reshape(DEPTH, m, PLE_DIM)
    rw = jnp.pad(router_w.astype(F32), ((0, 0), (0, LANES - N_EXPERTS)))
    rwh = rw.astype(BF16)
    rw2 = jnp.concatenate([rwh, (rw - rwh.astype(F32)).astype(BF16)], axis=1)
    rb = router_bias.astype(F32).reshape(N_EXPERTS, 1)
    bucket_ids = jnp.arange(N_BUCKETS, dtype=jnp.int32)
    block_ids = jnp.arange(n_blocks, dtype=jnp.int32)
    w_out_a_b, w_out_b_b, w_o_b = w_out_a.astype(BF16), w_out_b.astype(BF16), w_o.astype(BF16)
    wpg_b, wpp_b = ple_w_gate.astype(BF16), ple_w_proj.astype(BF16)
    wg_rows = exp_w_gate.reshape(DEPTH, N_EXPERTS * D_MODEL, D_EXPERT)
    wu_rows = exp_w_up.reshape(DEPTH, N_EXPERTS * D_MODEL, D_EXPERT)
    wd_rows = exp_w_down.reshape(DEPTH, N_EXPERTS * D_EXPERT, D_MODEL)

    for i in range(DEPTH):
        ug, wd_b = _inproj(xb, w_in, i, 0, 2 * A_WIDTH, jax.nn.gelu, cast=(wd_rows,))
        qkv = _inproj(xb, w_in, i, 2 * A_WIDTH, 3 * B_WIDTH, lambda a: a,
                      first_tile_scale=B_HEAD_DIM ** -0.5 * LOG2E)
        gates, wg_b, wu_b = _inproj(xb, w_in, i, 2 * A_WIDTH + 3 * B_WIDTH, 2 * D_MODEL, jax.nn.sigmoid,
                                    cast=(wg_rows, wu_rows))
        wg_b = wg_b.reshape(N_EXPERTS, D_MODEL, D_EXPERT)
        wu_b = wu_b.reshape(N_EXPERTS, D_MODEL, D_EXPERT)
        wd_b = wd_b.reshape(N_EXPERTS, D_EXPERT, D_MODEL)

        bias = jnp.repeat(jnp.transpose(gmlp_bs[i]), A_HEAD_DIM, axis=1)
        ya = _gmlp(ug, gmlp_ln_g[i].reshape(1, A_WIDTH), gmlp_ln_b[i].reshape(1, A_WIDTH),
                   gmlp_ws, i, bias, w_out_a_b)
        ob = _attention(qkv, batch, seq)

        x1, br, gwt, cnt = _merge(
            ob, ya, gates, xf, w_out_b_b, w_o_b, i,
            ln1_g[i].reshape(1, d), ln1_b[i].reshape(1, d), rw2, rb)

        counts = cnt[:N_BUCKETS, 0].astype(jnp.int32)
        nblk = jnp.maximum((counts + tm - 1) // tm, 1)
        blk_end = jnp.cumsum(nblk)
        pstart = (blk_end - nblk) * tm
        dest = jnp.sum(jnp.where(br[0][None, :] == bucket_ids[:, None], pstart[:, None], 0), axis=0) + br[1]
        zstart = pstart + counts // SUBLANES * SUBLANES
        blk_bucket = jnp.minimum(
            jnp.sum((block_ids[:, None] >= blk_end[None, :]).astype(jnp.int32), axis=1), N_BUCKETS - 1)
        blk_lo = jnp.asarray(BUCKET_LO)[blk_bucket]
        blk_hi = jnp.asarray(BUCKET_HI)[blk_bucket]
        n_valid = blk_end[-1:].astype(jnp.int32)

        base, xd = _ple_dispatch(dest, zstart, n_valid, x1, p_flat, i, gwt, wpg_b, wpp_b, xd_rows)
        y = _experts(blk_lo, blk_hi, n_valid, xd, wg_b, wu_b, wd_b, n_blocks)
        xf, xb = _final(dest, base, ln2_g[i].reshape(1, d), ln2_b[i].reshape(1, d), y)

    return xf.reshape(batch, seq, d)
```

```python
import functools

import jax
import jax.numpy as jnp
import numpy as np
from jax import lax
from jax.experimental import pallas as pl
from jax.experimental.pallas import tpu as pltpu

D_MODEL = 2048
DEPTH = 4
A_HEADS = 8
A_HEAD_DIM = 128
A_WIDTH = A_HEADS * A_HEAD_DIM
CHUNK = 128
B_HEADS = 8
B_HEAD_DIM = 128
B_WIDTH = B_HEADS * B_HEAD_DIM
N_EXPERTS = 16
N_GROUPS = 4
EXPERTS_PER_GROUP = N_EXPERTS // N_GROUPS
D_EXPERT = 1024
PLE_DIM = 256
ALPHA = (2 * DEPTH) ** 0.25
LN_EPS = 1e-5
LOG2E = 1.4426950408889634

LANES = 128
SUBLANES = 8
VMEM_LIMIT = 56 * 1024 * 1024
EXPERT_VMEM_LIMIT = 62 * 1024 * 1024

F32 = jnp.float32
BF16 = jnp.bfloat16

PAIRS = [(lo, hi) for lo in range(EXPERTS_PER_GROUP) for hi in range(lo + 1, EXPERTS_PER_GROUP)]
PAIRS_PER_GROUP = len(PAIRS)
N_BUCKETS = N_GROUPS * PAIRS_PER_GROUP
BUCKET_ROWS = 32
BUCKET_LO = np.array([g * EXPERTS_PER_GROUP + lo for g in range(N_GROUPS) for lo, _ in PAIRS], np.int32)
BUCKET_HI = np.array([g * EXPERTS_PER_GROUP + hi for g in range(N_GROUPS) for _, hi in PAIRS], np.int32)
PAIR_BASE = [PAIRS.index((lo, lo + 1)) for lo in range(EXPERTS_PER_GROUP - 1)]
XD_WIDTH = D_MODEL + LANES

INPROJ_TM = 1024
INPROJ_TN = 1024
INPROJ_CHUNKS = 8
GMLP_T = 512
ATTN_T = 256
ATTN_HEADS_PER_STEP = 8
ATTN_ROW_PARTS = 1
MERGE_T = 256
DISPATCH_T = 256
EXPERT_TM = 128
FINAL_T = 256
FINAL_SLOTS = 3


def _layer_norm(xf, gain, bias):
    mu = jnp.mean(xf, axis=-1, keepdims=True)
    xc = xf - mu
    var = jnp.mean(xc * xc, axis=-1, keepdims=True)
    return xc * lax.rsqrt(var + LN_EPS) * gain + bias


def _const_spec(shape):
    nd = len(shape)
    return pl.BlockSpec(shape, lambda *_: (0,) * nd, pipeline_mode=pl.Buffered(1))


def _layer_spec(shape, layer):
    nd = len(shape)
    return pl.BlockSpec((None,) + tuple(shape), lambda *_: (layer,) + (0,) * nd,
                        pipeline_mode=pl.Buffered(1))


def _inproj_kernel(x_ref, w_ref, *refs, act, first_tile_scale, n_cast):
    cast_src, o_ref, cast_dst, wb_ref = refs[:n_cast], refs[n_cast], refs[n_cast + 1:-1], refs[-1]

    @pl.when(pl.program_id(1) == 0)
    def _():
        wb_ref[...] = w_ref[...].astype(BF16)

    for src, dst in zip(cast_src, cast_dst):
        dst[...] = src[...].astype(BF16)

    rows = x_ref.shape[0] // INPROJ_CHUNKS
    for c in range(INPROJ_CHUNKS):
        rs = slice(c * rows, (c + 1) * rows)
        y = act(jnp.dot(x_ref[rs, :], wb_ref[...], preferred_element_type=F32))
        if first_tile_scale is not None:
            y = y * jnp.where(pl.program_id(0) == 0, first_tile_scale, 1.0)
        o_ref[rs, :] = y.astype(o_ref.dtype)


def _inproj(xb, w, layer, col0, ncols, act, first_tile_scale=None, cast=()):
    m, k = xb.shape
    tm, tn = INPROJ_TM, INPROJ_TN
    cb0 = col0 // tn
    n_i = m // tm
    steps = (ncols // tn) * n_i
    cast_in_specs, cast_out_specs, cast_shapes = [], [], []
    for arr in cast:
        _, rows, cols = arr.shape
        slab = rows // steps
        cast_in_specs.append(pl.BlockSpec((None, slab, cols), lambda j, i: (layer, j * n_i + i, 0)))
        cast_out_specs.append(pl.BlockSpec((slab, cols), lambda j, i: (j * n_i + i, 0)))
        cast_shapes.append(jax.ShapeDtypeStruct((rows, cols), BF16))
    out = pl.pallas_call(
        functools.partial(_inproj_kernel, act=act, first_tile_scale=first_tile_scale, n_cast=len(cast)),
        grid=(ncols // tn, n_i),
        in_specs=[
            pl.BlockSpec((tm, k), lambda j, i: (i, 0)),
            pl.BlockSpec((None, k, tn), lambda j, i: (layer, 0, cb0 + j), pipeline_mode=pl.Buffered(1)),
        ] + cast_in_specs,
        out_specs=[pl.BlockSpec((tm, tn), lambda j, i: (i, j))] + cast_out_specs,
        out_shape=[jax.ShapeDtypeStruct((m, ncols), BF16)] + cast_shapes,
        scratch_shapes=[pltpu.VMEM((k, tn), BF16)],
        compiler_params=pltpu.CompilerParams(
            dimension_semantics=("arbitrary", "arbitrary"), vmem_limit_bytes=VMEM_LIMIT),
        name="inproj",
    )(xb, w, *cast)
    return out if cast else out[0]


def _gmlp_kernel(u_ref, gv_ref, lng_ref, lnb_ref, ws_ref, bias_ref, wo_ref, o_ref, yin_ref):
    t = u_ref.shape[0]
    nc = t // CHUNK
    v = _layer_norm(gv_ref[...].astype(F32), lng_ref[...], lnb_ref[...]).astype(BF16)
    row = lax.broadcasted_iota(jnp.int32, (CHUNK, CHUNK), 0)
    col = lax.broadcasted_iota(jnp.int32, (CHUNK, CHUNK), 1)
    tril = row >= col
    for h in range(A_HEADS):
        hs = slice(h * A_HEAD_DIM, (h + 1) * A_HEAD_DIM)
        wsm = jnp.where(tril, ws_ref[h], 0.0).astype(BF16)
        vh = jnp.concatenate([v[c * CHUNK:(c + 1) * CHUNK, hs] for c in range(nc)], axis=1)
        f = jnp.dot(wsm, vh, preferred_element_type=F32)
        bh = bias_ref[:, hs]
        for c in range(nc):
            rs = slice(c * CHUNK, (c + 1) * CHUNK)
            fc = f[:, c * A_HEAD_DIM:(c + 1) * A_HEAD_DIM] + bh
            yin_ref[rs, hs] = (u_ref[rs, hs].astype(F32) * fc).astype(BF16)
    o_ref[...] = jnp.dot(yin_ref[...], wo_ref[...], preferred_element_type=F32).astype(o_ref.dtype)


def _gmlp(ug, lng, lnb, ws, layer, bias, wo):
    m = ug.shape[0]
    t = GMLP_T
    return pl.pallas_call(
        _gmlp_kernel,
        grid=(m // t,),
        in_specs=[
            pl.BlockSpec((t, A_WIDTH), lambda i: (i, 0)),
            pl.BlockSpec((t, A_WIDTH), lambda i: (i, 1)),
            _const_spec((1, A_WIDTH)),
            _const_spec((1, A_WIDTH)),
            _layer_spec((A_HEADS, CHUNK, CHUNK), layer),
            _const_spec((CHUNK, A_WIDTH)),
            _layer_spec((A_WIDTH, D_MODEL), layer),
        ],
        out_specs=pl.BlockSpec((t, D_MODEL), lambda i: (i, 0)),
        out_shape=jax.ShapeDtypeStruct((m, D_MODEL), BF16),
        scratch_shapes=[pltpu.VMEM((t, A_WIDTH), BF16)],
        compiler_params=pltpu.CompilerParams(
            dimension_semantics=("parallel",), vmem_limit_bytes=VMEM_LIMIT),
        name="gmlp",
    )(ug, ug, lng, lnb, ws, bias, wo)


def _attn_kernel(q_ref, k_ref, v_ref, o_ref):
    t = q_ref.shape[0]
    nh = q_ref.shape[1] // B_HEAD_DIM
    heads = range(nh)
    i = pl.program_id(2)
    row = lax.broadcasted_iota(jnp.int32, (t, t), 0)
    col = lax.broadcasted_iota(jnp.int32, (t, t), 1)
    neg_upper = jnp.where(row > col, -1.0, 0.0).astype(BF16)
    causal = col < row
    hs = [slice(h * B_HEAD_DIM, (h + 1) * B_HEAD_DIM) for h in heads]
    rp = t // ATTN_ROW_PARTS
    units = [(h, slice(r * rp, (r + 1) * rp)) for h in heads for r in range(ATTN_ROW_PARTS)]
    qs = [q_ref[rs, hs[h]] for h, rs in units]
    masks = [causal[rs, :] for _, rs in units]

    def tiles(off, state, diag):
        z = [lax.dot_general(qs[u], k_ref[pl.ds(off, t), hs[h]], (((1,), (1,)), ((), ())),
                             preferred_element_type=F32) for u, (h, _) in enumerate(units)]
        sp, logsig, packed = [], [], []
        for u in range(len(units)):
            neg_abs = pltpu.bitcast(pltpu.bitcast(z[u], jnp.uint32) | jnp.uint32(0x80000000), F32)
            s = jnp.maximum(z[u], 0.0) + jnp.log(1.0 + jnp.exp2(neg_abs)) * LOG2E
            logsig.append(z[u] - s)
            if diag:
                s = jnp.where(masks[u], s, 0.0)
            sp.append(s)
            packed.append(s.astype(BF16))
        suffix = [jnp.dot(packed[u], neg_upper, preferred_element_type=F32) for u in range(len(units))]
        a = []
        for u in range(len(units)):
            au = jnp.exp2(logsig[u] + suffix[u] + state[u][0])
            if diag:
                au = jnp.where(masks[u], au, 0.0)
            a.append(au.astype(BF16))
        return tuple(
            (state[u][0] - jnp.sum(sp[u], axis=1, keepdims=True),
             state[u][1] + jnp.dot(a[u], v_ref[pl.ds(off, t), hs[h]], preferred_element_type=F32))
            for u, (h, _) in enumerate(units))

    init = tuple((jnp.zeros((rp, 1), F32), jnp.zeros((rp, B_HEAD_DIM), F32)) for _ in units)
    state = tiles(pl.multiple_of(i * t, t), init, True)

    def body(jj, st):
        return tiles(pl.multiple_of((i - 1 - jj) * t, t), st, False)

    state = lax.fori_loop(0, i, body, state)
    for u, (h, rs) in enumerate(units):
        o_ref[rs, hs[h]] = state[u][1].astype(o_ref.dtype)


def _attention(qkv, batch, seq):
    m = qkv.shape[0]
    t = ATTN_T
    nq = seq // t
    w = ATTN_HEADS_PER_STEP * B_HEAD_DIM
    hb = B_WIDTH // w
    return pl.pallas_call(
        _attn_kernel,
        grid=(batch, hb, nq),
        in_specs=[
            pl.BlockSpec((t, w), lambda b, h, i: (b * nq + i, h)),
            pl.BlockSpec((seq, w), lambda b, h, i: (b, hb + h)),
            pl.BlockSpec((seq, w), lambda b, h, i: (b, 2 * hb + h)),
        ],
        out_specs=pl.BlockSpec((t, w), lambda b, h, i: (b * nq + i, h)),
        out_shape=jax.ShapeDtypeStruct((m, B_WIDTH), BF16),
        compiler_params=pltpu.CompilerParams(
            dimension_semantics=("parallel", "parallel", "parallel"), vmem_limit_bytes=VMEM_LIMIT),
        name="attn",
    )(qkv, qkv, qkv)


def _first_max(vals):
    best, idx = vals[0], jnp.zeros(vals[0].shape, jnp.int32)
    for j in range(1, len(vals)):
        better = vals[j] > best
        best = jnp.where(better, vals[j], best)
        idx = jnp.where(better, j, idx)
    return best, idx


def _pick(idx, vals):
    out = vals[0]
    for j in range(1, len(vals)):
        out = jnp.where(idx == j, vals[j], out)
    return out


def _merge_kernel(ob_ref, ya_ref, sga_ref, sgb_ref, x_ref, wob_ref, wo_ref, g_ref, b_ref,
                  rw2_ref, rb_ref, x1_ref, br_ref, gwt_ref, cnt_ref, base_ref, logit_ref):
    t = x_ref.shape[0]
    step = pl.program_id(0)

    @pl.when(step == 0)
    def _():
        base_ref[...] = jnp.zeros_like(base_ref)
        logit_ref[...] = jnp.zeros_like(logit_ref)

    live = jnp.where(step > 0, 1.0, 0.0)

    lt = jnp.transpose(logit_ref[...])[:N_EXPERTS, :]
    aff = jax.nn.sigmoid(lt)
    sel = aff + rb_ref[...]
    sel_rows = [sel[e:e + 1, :] for e in range(N_EXPERTS)]
    aff_rows = [aff[e:e + 1, :] for e in range(N_EXPERTS)]
    gscore = []
    for g in range(N_GROUPS):
        s0, s1, s2, s3 = sel_rows[4 * g:4 * g + 4]
        hi1, lo1 = jnp.maximum(s0, s1), jnp.minimum(s0, s1)
        hi2, lo2 = jnp.maximum(s2, s3), jnp.minimum(s2, s3)
        top1 = jnp.maximum(hi1, hi2)
        top2 = jnp.maximum(jnp.minimum(hi1, hi2), jnp.maximum(lo1, lo2))
        gscore.append(top1 + top2)
    _, gidx = _first_max(gscore)
    within = [_pick(gidx, [sel_rows[4 * g + j] for g in range(N_GROUPS)])
              for j in range(EXPERTS_PER_GROUP)]
    awithin = [_pick(gidx, [aff_rows[4 * g + j] for g in range(N_GROUPS)])
               for j in range(EXPERTS_PER_GROUP)]
    _, i0 = _first_max(within)
    masked = [jnp.where(i0 == j, -jnp.inf, within[j]) for j in range(EXPERTS_PER_GROUP)]
    _, i1 = _first_max(masked)
    a0, a1 = _pick(i0, awithin), _pick(i1, awithin)
    denom = a0 + a1
    w0, w1 = a0 / denom, a1 / denom

    swap = i1 < i0
    lo = jnp.where(swap, i1, i0)
    hi = jnp.where(swap, i0, i1)
    w_lo = jnp.where(swap, w1, w0)
    w_hi = jnp.where(swap, w0, w1)
    pair = _pick(lo, PAIR_BASE) + (hi - lo - 1)
    bucket = gidx * PAIRS_PER_GROUP + pair

    biota = lax.broadcasted_iota(jnp.int32, (BUCKET_ROWS, t), 0)
    oh = jnp.where(biota == bucket, 1.0, 0.0)
    row = lax.broadcasted_iota(jnp.int32, (t, t), 0)
    col = lax.broadcasted_iota(jnp.int32, (t, t), 1)
    before = jnp.where(row < col, 1.0, 0.0).astype(BF16)
    tot = base_ref[:, 0:1] + jnp.dot(oh.astype(BF16), before, preferred_element_type=F32)
    rank = jnp.sum(oh * tot, axis=0, keepdims=True)
    slot = lax.broadcasted_iota(jnp.int32, (2, t), 0)
    br_ref[...] = jnp.where(slot == 0, bucket, rank.astype(jnp.int32))
    newbase = base_ref[...] + live * jnp.sum(oh, axis=1, keepdims=True)
    base_ref[...] = newbase
    cnt_ref[...] = newbase

    wrow = lax.broadcasted_iota(jnp.int32, (LANES, t), 0)
    gw = jnp.where(wrow == 0, w_lo, jnp.where(wrow == 1, w_hi, 0.0))
    gwt_ref[...] = jnp.transpose(gw)

    yb = jnp.dot(ob_ref[...], wob_ref[...], preferred_element_type=F32)
    mixin = sga_ref[...].astype(F32) * ya_ref[...].astype(F32) + sgb_ref[...].astype(F32) * yb
    mix = jnp.dot(mixin.astype(BF16), wo_ref[...], preferred_element_type=F32)
    x1 = _layer_norm(ALPHA * x_ref[...] + mix, g_ref[...], b_ref[...])
    x1_ref[...] = x1
    x_hi = x1.astype(BF16)
    x_lo = (x1 - x_hi.astype(F32)).astype(BF16)
    r_hi = jnp.dot(x_hi, rw2_ref[...], preferred_element_type=F32)
    r_lo = jnp.dot(x_lo, rw2_ref[...], preferred_element_type=F32)
    logit_ref[...] = ((r_hi[:, :LANES] + r_hi[:, LANES:])
                      + (r_lo[:, :LANES] + r_lo[:, LANES:]))


def _merge(ob, ya, gates, xf, wob, wo, layer, g, b, rw2, rb):
    m = xf.shape[0]
    t = MERGE_T
    last = m // t - 1

    def cur(col):
        return lambda i: (jnp.minimum(i, last), col)

    return pl.pallas_call(
        _merge_kernel,
        grid=(m // t + 1,),
        in_specs=[
            pl.BlockSpec((t, B_WIDTH), cur(0)),
            pl.BlockSpec((t, D_MODEL), cur(0)),
            pl.BlockSpec((t, D_MODEL), cur(0)),
            pl.BlockSpec((t, D_MODEL), cur(1)),
            pl.BlockSpec((t, D_MODEL), cur(0)),
            _layer_spec((B_WIDTH, D_MODEL), layer),
            _layer_spec((D_MODEL, D_MODEL), layer),
            _const_spec((1, D_MODEL)),
            _const_spec((1, D_MODEL)),
            _const_spec((D_MODEL, 2 * LANES)),
            _const_spec((N_EXPERTS, 1)),
        ],
        out_specs=[
            pl.BlockSpec((t, D_MODEL), cur(0)),
            pl.BlockSpec((2, t), lambda i: (0, jnp.maximum(i - 1, 0))),
            pl.BlockSpec((t, LANES), lambda i: (jnp.maximum(i - 1, 0), 0)),
            pl.BlockSpec((BUCKET_ROWS, LANES), lambda i: (0, 0)),
        ],
        out_shape=[
            jax.ShapeDtypeStruct((m, D_MODEL), F32),
            jax.ShapeDtypeStruct((2, m), jnp.int32),
            jax.ShapeDtypeStruct((m, LANES), F32),
            jax.ShapeDtypeStruct((BUCKET_ROWS, LANES), F32),
        ],
        scratch_shapes=[pltpu.VMEM((BUCKET_ROWS, LANES), F32), pltpu.VMEM((t, LANES), F32)],
        compiler_params=pltpu.CompilerParams(
            dimension_semantics=("arbitrary",), vmem_limit_bytes=VMEM_LIMIT),
        name="merge",
    )(ob, ya, gates, gates, xf, wob, wo, g, b, rw2, rb)


def _dispatch_row_copy(xext_ref, xd_ref, sem, src_row, dst_row):
    return pltpu.make_async_copy(xext_ref.at[pl.ds(src_row, 1), :], xd_ref.at[pl.ds(dst_row, 1), :], sem)


def _zero_fill_copy(zero_ref, xd_ref, sem, dst_row):
    rows = zero_ref.shape[0]
    return pltpu.make_async_copy(zero_ref, xd_ref.at[pl.ds(pl.multiple_of(dst_row, SUBLANES), rows), :], sem)


def _ple_dispatch_kernel(dest_ref, zstart_ref, nv_ref, x1_ref, p_ref, gwt_ref, wpg_ref, wpp_ref,
                         base_ref, xd_ref, xext_ref, zero_ref, sems, zsem):
    t = x1_ref.shape[0]
    first = pl.program_id(0) * t

    @pl.when(pl.program_id(0) == 0)
    def _():
        zero_ref[...] = jnp.zeros_like(zero_ref)
        for k in range(N_BUCKETS):
            _zero_fill_copy(zero_ref, xd_ref, zsem, zstart_ref[k]).start()
        for k in range(N_BUCKETS):
            _zero_fill_copy(zero_ref, xd_ref, zsem, zstart_ref[k]).wait()
        rows = zero_ref.shape[0]
        total_blocks = xd_ref.shape[0] // rows

        def fill(b, c):
            _zero_fill_copy(zero_ref, xd_ref, zsem, b * rows).start()
            return c

        def fill_done(b, c):
            _zero_fill_copy(zero_ref, xd_ref, zsem, b * rows).wait()
            return c

        lax.fori_loop(nv_ref[0], total_blocks, fill, 0)
        lax.fori_loop(nv_ref[0], total_blocks, fill_done, 0)

    step = pl.program_id(0)
    slot = step % 2
    x1 = x1_ref[...]
    xext_ref[slot, :, :D_MODEL] = x1
    xext_ref[slot, :, D_MODEL:] = gwt_ref[...]

    for r in range(t):
        _dispatch_row_copy(xext_ref.at[slot], xd_ref, sems.at[slot], r, dest_ref[first + r]).start()

    gate = jax.nn.sigmoid(jnp.dot(x1.astype(BF16), wpg_ref[...], preferred_element_type=F32))
    proj = jnp.dot(p_ref[...].astype(BF16), wpp_ref[...], preferred_element_type=F32)
    base_ref[...] = ALPHA * x1 + gate * proj

    def drain(s):
        for r in range(t):
            _dispatch_row_copy(xext_ref.at[s], xd_ref, sems.at[s], 0, 0).wait()

    @pl.when(step > 0)
    def _():
        drain(1 - slot)

    @pl.when(step == pl.num_programs(0) - 1)
    def _():
        drain(slot)


def _ple_dispatch(dest, zstart, n_valid, x1, p, layer, gwt, wpg, wpp, xd_rows):
    m, d = x1.shape
    t = DISPATCH_T
    return pl.pallas_call(
        _ple_dispatch_kernel,
        grid_spec=pltpu.PrefetchScalarGridSpec(
            num_scalar_prefetch=3,
            grid=(m // t,),
            in_specs=[
                pl.BlockSpec((t, d), lambda i, *_: (i, 0)),
                pl.BlockSpec((None, t, PLE_DIM), lambda i, *_: (layer, i, 0)),
                pl.BlockSpec((t, LANES), lambda i, *_: (i, 0)),
                _layer_spec((d, d), layer),
                _layer_spec((PLE_DIM, d), layer),
            ],
            out_specs=[
                pl.BlockSpec((t, d), lambda i, *_: (i, 0)),
                pl.BlockSpec(memory_space=pl.ANY),
            ],
            scratch_shapes=[
                pltpu.VMEM((2, t, XD_WIDTH), F32),
                pltpu.VMEM((EXPERT_TM, XD_WIDTH), F32),
                pltpu.SemaphoreType.DMA((2,)),
                pltpu.SemaphoreType.DMA(()),
            ],
        ),
        out_shape=[jax.ShapeDtypeStruct((m, d), F32), jax.ShapeDtypeStruct((xd_rows, XD_WIDTH), F32)],
        compiler_params=pltpu.CompilerParams(
            dimension_semantics=("arbitrary",), vmem_limit_bytes=VMEM_LIMIT),
        name="ple_dispatch",
    )(dest, zstart, n_valid, x1, p, gwt, wpg, wpp)


def _ffn(xb, wg_ref, wu_ref, wd_ref):
    g = jnp.dot(xb, wg_ref[...], preferred_element_type=F32)
    u = jnp.dot(xb, wu_ref[...], preferred_element_type=F32)
    h = (jax.nn.silu(g) * u).astype(BF16)
    return jnp.dot(h, wd_ref[...], preferred_element_type=F32)


def _expert_kernel(lo_ref, hi_ref, nv_ref, x_ref, wgl_ref, wul_ref, wdl_ref, wgh_ref, wuh_ref, wdh_ref,
                   y_ref):
    del lo_ref, hi_ref
    b = pl.program_id(0)

    @pl.when(b < nv_ref[0])
    def _():
        xb = x_ref[:, :D_MODEL].astype(BF16)
        w_lo = x_ref[:, D_MODEL:D_MODEL + 1]
        w_hi = x_ref[:, D_MODEL + 1:D_MODEL + 2]
        y_ref[...] = (w_lo * _ffn(xb, wgl_ref, wul_ref, wdl_ref)
                      + w_hi * _ffn(xb, wgh_ref, wuh_ref, wdh_ref))

    @pl.when(b >= nv_ref[0])
    def _():
        y_ref[...] = jnp.zeros_like(y_ref)


def _experts(blk_lo, blk_hi, n_valid, xd, wg, wu, wd, n_blocks):
    d = D_MODEL
    tm = EXPERT_TM

    def x_map(b, lo, hi, nv):
        return (jnp.minimum(b, jnp.maximum(nv[0] - 1, 0)), 0)

    def lo_map(b, lo, hi, nv):
        return (lo[b], 0, 0)

    def hi_map(b, lo, hi, nv):
        return (hi[b], 0, 0)

    return pl.pallas_call(
        _expert_kernel,
        grid_spec=pltpu.PrefetchScalarGridSpec(
            num_scalar_prefetch=3,
            grid=(n_blocks,),
            in_specs=[
                pl.BlockSpec((tm, XD_WIDTH), x_map),
                pl.BlockSpec((None, d, D_EXPERT), lo_map),
                pl.BlockSpec((None, d, D_EXPERT), lo_map),
                pl.BlockSpec((None, D_EXPERT, d), lo_map),
                pl.BlockSpec((None, d, D_EXPERT), hi_map),
                pl.BlockSpec((None, d, D_EXPERT), hi_map),
                pl.BlockSpec((None, D_EXPERT, d), hi_map),
            ],
            out_specs=pl.BlockSpec((tm, d), lambda b, lo, hi, nv: (b, 0)),
        ),
        out_shape=jax.ShapeDtypeStruct((n_blocks * tm, d), F32),
        compiler_params=pltpu.CompilerParams(
            dimension_semantics=("arbitrary",), vmem_limit_bytes=EXPERT_VMEM_LIMIT),
        name="experts",
    )(blk_lo, blk_hi, n_valid, xd, wg, wu, wd, wg, wu, wd)


def _combine_row_copy(y_ref, buf_ref, sems, slot, r, src_row):
    return pltpu.make_async_copy(y_ref.at[pl.ds(src_row, 1), :], buf_ref.at[slot, pl.ds(r, 1), :],
                                 sems.at[slot])


def _final_kernel(dest_ref, base_ref, g_ref, b_ref, y_ref, xo_ref, xob_ref, buf_ref, sems):
    t = base_ref.shape[0]
    i = pl.program_id(0)
    last = pl.num_programs(0) - 1

    def issue(tile, slot):
        for r in range(t):
            _combine_row_copy(y_ref, buf_ref, sems, slot, r, dest_ref[tile * t + r]).start()

    def drain(slot):
        for r in range(t):
            _combine_row_copy(y_ref, buf_ref, sems, slot, 0, 0).wait()

    @pl.when(i == 0)
    def _():
        issue(0, 0)
        issue(jnp.minimum(1, last), 1)

    slot = i % FINAL_SLOTS
    drain(slot)
    xo_ref[...] = base_ref[...] + buf_ref[slot]
    issue(jnp.minimum(i + 2, last), (i + 2) % FINAL_SLOTS)
    x2 = _layer_norm(xo_ref[...], g_ref[...], b_ref[...])
    xo_ref[...] = x2
    xob_ref[...] = x2.astype(BF16)

    @pl.when(i == last)
    def _():
        drain((i + 1) % FINAL_SLOTS)
        drain((i + 2) % FINAL_SLOTS)


def _final(dest, base, g, b, y):
    m, d = base.shape
    t = FINAL_T
    return pl.pallas_call(
        _final_kernel,
        grid_spec=pltpu.PrefetchScalarGridSpec(
            num_scalar_prefetch=1,
            grid=(m // t,),
            in_specs=[
                pl.BlockSpec((t, d), lambda i, dest: (i, 0)),
                pl.BlockSpec((1, d), lambda i, dest: (0, 0), pipeline_mode=pl.Buffered(1)),
                pl.BlockSpec((1, d), lambda i, dest: (0, 0), pipeline_mode=pl.Buffered(1)),
                pl.BlockSpec(memory_space=pl.ANY),
            ],
            out_specs=[
                pl.BlockSpec((t, d), lambda i, dest: (i, 0)),
                pl.BlockSpec((t, d), lambda i, dest: (i, 0)),
            ],
            scratch_shapes=[pltpu.VMEM((FINAL_SLOTS, t, d), F32), pltpu.SemaphoreType.DMA((FINAL_SLOTS,))],
        ),
        out_shape=[jax.ShapeDtypeStruct((m, d), F32), jax.ShapeDtypeStruct((m, d), BF16)],
        compiler_params=pltpu.CompilerParams(
            dimension_semantics=("arbitrary",), vmem_limit_bytes=VMEM_LIMIT),
        name="final",
    )(dest, base, g, b, y)


def kernel(x, p, w_in, gmlp_ln_g, gmlp_ln_b, gmlp_ws, gmlp_bs, w_out_a, w_out_b, w_o, ln1_g, ln1_b,
           router_w, router_bias, exp_w_gate, exp_w_up, exp_w_down, ple_w_gate, ple_w_proj,
           ln2_g, ln2_b):
    batch, seq, d = x.shape
    m = batch * seq
    tm = EXPERT_TM
    n_blocks = m // tm + N_BUCKETS
    xd_rows = (n_blocks + 1) * tm

    xf = x.reshape(m, d)
    xb = xf.astype(BF16)
    p_flat = p.reshape(DEPTH, m, PLE_DIM)
    rw = jnp.pad(router_w.astype(F32), ((0, 0), (0, LANES - N_EXPERTS)))
    rwh = rw.astype(BF16)
    rw2 = jnp.concatenate([rwh, (rw - rwh.astype(F32)).astype(BF16)], axis=1)
    rb = router_bias.astype(F32).reshape(N_EXPERTS, 1)
    bucket_ids = jnp.arange(N_BUCKETS, dtype=jnp.int32)
    block_ids = jnp.arange(n_blocks, dtype=jnp.int32)
    w_out_a_b, w_out_b_b, w_o_b = w_out_a.astype(BF16), w_out_b.astype(BF16), w_o.astype(BF16)
    wpg_b, wpp_b = ple_w_gate.astype(BF16), ple_w_proj.astype(BF16)
    wg_rows = exp_w_gate.reshape(DEPTH, N_EXPERTS * D_MODEL, D_EXPERT)
    wu_rows = exp_w_up.reshape(DEPTH, N_EXPERTS * D_MODEL, D_EXPERT)
    wd_rows = exp_w_down.reshape(DEPTH, N_EXPERTS * D_EXPERT, D_MODEL)

    for i in range(DEPTH):
        ug, wd_b = _inproj(xb, w_in, i, 0, 2 * A_WIDTH, jax.nn.gelu, cast=(wd_rows,))
        qkv = _inproj(xb, w_in, i, 2 * A_WIDTH, 3 * B_WIDTH, lambda a: a,
                      first_tile_scale=B_HEAD_DIM ** -0.5 * LOG2E)
        gates, wg_b, wu_b = _inproj(xb, w_in, i, 2 * A_WIDTH + 3 * B_WIDTH, 2 * D_MODEL, jax.nn.sigmoid,
                                    cast=(wg_rows, wu_rows))
        wg_b = wg_b.reshape(N_EXPERTS, D_MODEL, D_EXPERT)
        wu_b = wu_b.reshape(N_EXPERTS, D_MODEL, D_EXPERT)
        wd_b = wd_b.reshape(N_EXPERTS, D_EXPERT, D_MODEL)

        bias = jnp.repeat(jnp.transpose(gmlp_bs[i]), A_HEAD_DIM, axis=1)
        ya = _gmlp(ug, gmlp_ln_g[i].reshape(1, A_WIDTH), gmlp_ln_b[i].reshape(1, A_WIDTH),
                   gmlp_ws, i, bias, w_out_a_b)
        ob = _attention(qkv, batch, seq)

        x1, br, gwt, cnt = _merge(
            ob, ya, gates, xf, w_out_b_b, w_o_b, i,
            ln1_g[i].reshape(1, d), ln1_b[i].reshape(1, d), rw2, rb)

        counts = cnt[:N_BUCKETS, 0].astype(jnp.int32)
        nblk = jnp.maximum((counts + tm - 1) // tm, 1)
        blk_end = jnp.cumsum(nblk)
        pstart = (blk_end - nblk) * tm
        dest = jnp.sum(jnp.where(br[0][None, :] == bucket_ids[:, None], pstart[:, None], 0), axis=0) + br[1]
        zstart = pstart + counts // SUBLANES * SUBLANES
        blk_bucket = jnp.minimum(
            jnp.sum((block_ids[:, None] >= blk_end[None, :]).astype(jnp.int32), axis=1), N_BUCKETS - 1)
        blk_lo = jnp.asarray(BUCKET_LO)[blk_bucket]
        blk_hi = jnp.asarray(BUCKET_HI)[blk_bucket]
        n_valid = blk_end[-1:].astype(jnp.int32)

        base, xd = _ple_dispatch(dest, zstart, n_valid, x1, p_flat, i, gwt, wpg_b, wpp_b, xd_rows)
        y = _experts(blk_lo, blk_hi, n_valid, xd, wg_b, wu_b, wd_b, n_blocks)
        xf, xb = _final(dest, base, ln2_g[i].reshape(1, d), ln2_b[i].reshape(1, d), y)

    return xf.reshape(batch, seq, d)
```

```python
import functools

import jax
import jax.numpy as jnp
import numpy as np
from jax import lax
from jax.experimental import pallas as pl
from jax.experimental.pallas import tpu as pltpu

D_MODEL = 2048
DEPTH = 4
A_HEADS = 8
A_HEAD_DIM = 128
A_WIDTH = A_HEADS * A_HEAD_DIM
CHUNK = 128
B_HEADS = 8
B_HEAD_DIM = 128
B_WIDTH = B_HEADS * B_HEAD_DIM
N_EXPERTS = 16
N_GROUPS = 4
EXPERTS_PER_GROUP = N_EXPERTS // N_GROUPS
D_EXPERT = 1024
PLE_DIM = 256
ALPHA = (2 * DEPTH) ** 0.25
LN_EPS = 1e-5
LOG2E = 1.4426950408889634

LANES = 128
SUBLANES = 8
VMEM_LIMIT = 56 * 1024 * 1024
EXPERT_VMEM_LIMIT = 62 * 1024 * 1024

F32 = jnp.float32
BF16 = jnp.bfloat16

PAIRS = [(lo, hi) for lo in range(EXPERTS_PER_GROUP) for hi in range(lo + 1, EXPERTS_PER_GROUP)]
PAIRS_PER_GROUP = len(PAIRS)
N_BUCKETS = N_GROUPS * PAIRS_PER_GROUP
BUCKET_ROWS = 32
BUCKET_LO = np.array([g * EXPERTS_PER_GROUP + lo for g in range(N_GROUPS) for lo, _ in PAIRS], np.int32)
BUCKET_HI = np.array([g * EXPERTS_PER_GROUP + hi for g in range(N_GROUPS) for _, hi in PAIRS], np.int32)
PAIR_BASE = [PAIRS.index((lo, lo + 1)) for lo in range(EXPERTS_PER_GROUP - 1)]
XD_WIDTH = D_MODEL + LANES

INPROJ_TM = 1024
INPROJ_TN = 1024
INPROJ_CHUNKS = 4
GMLP_T = 512
ATTN_T = 256
ATTN_HEADS_PER_STEP = 8
ATTN_ROW_PARTS = 1
MERGE_T = 256
DISPATCH_T = 256
EXPERT_TM = 128
FINAL_T = 256
FINAL_SLOTS = 3
DMA_PRIORITIES = 2


def _layer_norm(xf, gain, bias):
    mu = jnp.mean(xf, axis=-1, keepdims=True)
    xc = xf - mu
    var = jnp.mean(xc * xc, axis=-1, keepdims=True)
    return xc * lax.rsqrt(var + LN_EPS) * gain + bias


def _const_spec(shape):
    nd = len(shape)
    return pl.BlockSpec(shape, lambda *_: (0,) * nd, pipeline_mode=pl.Buffered(1))


def _layer_spec(shape, layer):
    nd = len(shape)
    return pl.BlockSpec((None,) + tuple(shape), lambda *_: (layer,) + (0,) * nd,
                        pipeline_mode=pl.Buffered(1))


def _inproj_kernel(x_ref, w_ref, *refs, act, first_tile_scale, n_cast):
    cast_src, o_ref, cast_dst, wb_ref = refs[:n_cast], refs[n_cast], refs[n_cast + 1:-1], refs[-1]

    @pl.when(pl.program_id(1) == 0)
    def _():
        wb_ref[...] = w_ref[...].astype(BF16)

    for src, dst in zip(cast_src, cast_dst):
        dst[...] = src[...].astype(BF16)

    rows = x_ref.shape[0] // INPROJ_CHUNKS
    for c in range(INPROJ_CHUNKS):
        rs = slice(c * rows, (c + 1) * rows)
        y = act(jnp.dot(x_ref[rs, :], wb_ref[...], preferred_element_type=F32))
        if first_tile_scale is not None:
            y = y * jnp.where(pl.program_id(0) == 0, first_tile_scale, 1.0)
        o_ref[rs, :] = y.astype(o_ref.dtype)


def _inproj(xb, w, layer, col0, ncols, act, first_tile_scale=None, cast=()):
    m, k = xb.shape
    tm, tn = INPROJ_TM, INPROJ_TN
    cb0 = col0 // tn
    n_i = m // tm
    steps = (ncols // tn) * n_i
    cast_in_specs, cast_out_specs, cast_shapes = [], [], []
    for arr in cast:
        _, rows, cols = arr.shape
        slab = rows // steps
        cast_in_specs.append(pl.BlockSpec((None, slab, cols), lambda j, i: (layer, j * n_i + i, 0)))
        cast_out_specs.append(pl.BlockSpec((slab, cols), lambda j, i: (j * n_i + i, 0)))
        cast_shapes.append(jax.ShapeDtypeStruct((rows, cols), BF16))
    out = pl.pallas_call(
        functools.partial(_inproj_kernel, act=act, first_tile_scale=first_tile_scale, n_cast=len(cast)),
        grid=(ncols // tn, n_i),
        in_specs=[
            pl.BlockSpec((tm, k), lambda j, i: (i, 0)),
            pl.BlockSpec((None, k, tn), lambda j, i: (layer, 0, cb0 + j), pipeline_mode=pl.Buffered(1)),
        ] + cast_in_specs,
        out_specs=[pl.BlockSpec((tm, tn), lambda j, i: (i, j))] + cast_out_specs,
        out_shape=[jax.ShapeDtypeStruct((m, ncols), BF16)] + cast_shapes,
        scratch_shapes=[pltpu.VMEM((k, tn), BF16)],
        compiler_params=pltpu.CompilerParams(
            dimension_semantics=("arbitrary", "arbitrary"), vmem_limit_bytes=VMEM_LIMIT),
        name="inproj",
    )(xb, w, *cast)
    return out if cast else out[0]


def _gmlp_kernel(u_ref, gv_ref, lng_ref, lnb_ref, ws_ref, bias_ref, wo_ref, o_ref, yin_ref):
    t = u_ref.shape[0]
    nc = t // CHUNK
    v = _layer_norm(gv_ref[...].astype(F32), lng_ref[...], lnb_ref[...]).astype(BF16)
    row = lax.broadcasted_iota(jnp.int32, (CHUNK, CHUNK), 0)
    col = lax.broadcasted_iota(jnp.int32, (CHUNK, CHUNK), 1)
    tril = row >= col
    for h in range(A_HEADS):
        hs = slice(h * A_HEAD_DIM, (h + 1) * A_HEAD_DIM)
        wsm = jnp.where(tril, ws_ref[h], 0.0).astype(BF16)
        vh = jnp.concatenate([v[c * CHUNK:(c + 1) * CHUNK, hs] for c in range(nc)], axis=1)
        f = jnp.dot(wsm, vh, preferred_element_type=F32)
        bh = bias_ref[:, hs]
        for c in range(nc):
            rs = slice(c * CHUNK, (c + 1) * CHUNK)
            fc = f[:, c * A_HEAD_DIM:(c + 1) * A_HEAD_DIM] + bh
            yin_ref[rs, hs] = (u_ref[rs, hs].astype(F32) * fc).astype(BF16)
    o_ref[...] = jnp.dot(yin_ref[...], wo_ref[...], preferred_element_type=F32).astype(o_ref.dtype)


def _gmlp(ug, lng, lnb, ws, layer, bias, wo):
    m = ug.shape[0]
    t = GMLP_T
    return pl.pallas_call(
        _gmlp_kernel,
        grid=(m // t,),
        in_specs=[
            pl.BlockSpec((t, A_WIDTH), lambda i: (i, 0)),
            pl.BlockSpec((t, A_WIDTH), lambda i: (i, 1)),
            _const_spec((1, A_WIDTH)),
            _const_spec((1, A_WIDTH)),
            _layer_spec((A_HEADS, CHUNK, CHUNK), layer),
            _const_spec((CHUNK, A_WIDTH)),
            _const_spec((A_WIDTH, D_MODEL)),
        ],
        out_specs=pl.BlockSpec((t, D_MODEL), lambda i: (i, 0)),
        out_shape=jax.ShapeDtypeStruct((m, D_MODEL), BF16),
        scratch_shapes=[pltpu.VMEM((t, A_WIDTH), BF16)],
        compiler_params=pltpu.CompilerParams(
            dimension_semantics=("parallel",), vmem_limit_bytes=VMEM_LIMIT),
        name="gmlp",
    )(ug, ug, lng, lnb, ws, bias, wo)


def _attn_kernel(q_ref, k_ref, v_ref, o_ref):
    t = q_ref.shape[0]
    nh = q_ref.shape[1] // B_HEAD_DIM
    heads = range(nh)
    i = pl.program_id(2)
    row = lax.broadcasted_iota(jnp.int32, (t, t), 0)
    col = lax.broadcasted_iota(jnp.int32, (t, t), 1)
    neg_upper = jnp.where(row > col, -1.0, 0.0).astype(BF16)
    causal = col < row
    hs = [slice(h * B_HEAD_DIM, (h + 1) * B_HEAD_DIM) for h in heads]
    rp = t // ATTN_ROW_PARTS
    units = [(h, slice(r * rp, (r + 1) * rp)) for h in heads for r in range(ATTN_ROW_PARTS)]
    qs = [q_ref[rs, hs[h]] for h, rs in units]
    masks = [causal[rs, :] for _, rs in units]

    def tiles(off, state, diag):
        z = [lax.dot_general(qs[u], k_ref[pl.ds(off, t), hs[h]], (((1,), (1,)), ((), ())),
                             preferred_element_type=F32) for u, (h, _) in enumerate(units)]
        sp, logsig, packed = [], [], []
        for u in range(len(units)):
            neg_abs = pltpu.bitcast(pltpu.bitcast(z[u], jnp.uint32) | jnp.uint32(0x80000000), F32)
            s = jnp.maximum(z[u], 0.0) + jnp.log(1.0 + jnp.exp2(neg_abs)) * LOG2E
            logsig.append(z[u] - s)
            if diag:
                s = jnp.where(masks[u], s, 0.0)
            sp.append(s)
            packed.append(s.astype(BF16))
        suffix = [jnp.dot(packed[u], neg_upper, preferred_element_type=F32) for u in range(len(units))]
        a = []
        for u in range(len(units)):
            au = jnp.exp2(logsig[u] + suffix[u] + state[u][0])
            if diag:
                au = jnp.where(masks[u], au, 0.0)
            a.append(au.astype(BF16))
        return tuple(
            (state[u][0] - jnp.sum(sp[u], axis=1, keepdims=True),
             state[u][1] + jnp.dot(a[u], v_ref[pl.ds(off, t), hs[h]], preferred_element_type=F32))
            for u, (h, _) in enumerate(units))

    init = tuple((jnp.zeros((rp, 1), F32), jnp.zeros((rp, B_HEAD_DIM), F32)) for _ in units)
    state = tiles(pl.multiple_of(i * t, t), init, True)

    def body(jj, st):
        return tiles(pl.multiple_of((i - 1 - jj) * t, t), st, False)

    state = lax.fori_loop(0, i, body, state)
    for u, (h, rs) in enumerate(units):
        o_ref[rs, hs[h]] = state[u][1].astype(o_ref.dtype)


def _attention(qkv, batch, seq):
    m = qkv.shape[0]
    t = ATTN_T
    nq = seq // t
    w = ATTN_HEADS_PER_STEP * B_HEAD_DIM
    hb = B_WIDTH // w
    return pl.pallas_call(
        _attn_kernel,
        grid=(batch, hb, nq),
        in_specs=[
            pl.BlockSpec((t, w), lambda b, h, i: (b * nq + i, h)),
            pl.BlockSpec((seq, w), lambda b, h, i: (b, hb + h)),
            pl.BlockSpec((seq, w), lambda b, h, i: (b, 2 * hb + h)),
        ],
        out_specs=pl.BlockSpec((t, w), lambda b, h, i: (b * nq + i, h)),
        out_shape=jax.ShapeDtypeStruct((m, B_WIDTH), BF16),
        compiler_params=pltpu.CompilerParams(
            dimension_semantics=("parallel", "parallel", "parallel"), vmem_limit_bytes=VMEM_LIMIT),
        name="attn",
    )(qkv, qkv, qkv)


def _first_max(vals):
    best, idx = vals[0], jnp.zeros(vals[0].shape, jnp.int32)
    for j in range(1, len(vals)):
        better = vals[j] > best
        best = jnp.where(better, vals[j], best)
        idx = jnp.where(better, j, idx)
    return best, idx


def _pick(idx, vals):
    out = vals[0]
    for j in range(1, len(vals)):
        out = jnp.where(idx == j, vals[j], out)
    return out


def _merge_kernel(ob_ref, ya_ref, sga_ref, sgb_ref, x_ref, wob_ref, wo_ref, g_ref, b_ref,
                  rw2_ref, rb_ref, x1_ref, br_ref, gwt_ref, cnt_ref, base_ref, logit_ref):
    t = x_ref.shape[0]
    step = pl.program_id(0)

    @pl.when(step == 0)
    def _():
        base_ref[...] = jnp.zeros_like(base_ref)
        logit_ref[...] = jnp.zeros_like(logit_ref)

    live = jnp.where(step > 0, 1.0, 0.0)

    lt = jnp.transpose(logit_ref[...])[:N_EXPERTS, :]
    aff = jax.nn.sigmoid(lt)
    sel = aff + rb_ref[...]
    sel_rows = [sel[e:e + 1, :] for e in range(N_EXPERTS)]
    aff_rows = [aff[e:e + 1, :] for e in range(N_EXPERTS)]
    gscore = []
    for g in range(N_GROUPS):
        s0, s1, s2, s3 = sel_rows[4 * g:4 * g + 4]
        hi1, lo1 = jnp.maximum(s0, s1), jnp.minimum(s0, s1)
        hi2, lo2 = jnp.maximum(s2, s3), jnp.minimum(s2, s3)
        top1 = jnp.maximum(hi1, hi2)
        top2 = jnp.maximum(jnp.minimum(hi1, hi2), jnp.maximum(lo1, lo2))
        gscore.append(top1 + top2)
    _, gidx = _first_max(gscore)
    within = [_pick(gidx, [sel_rows[4 * g + j] for g in range(N_GROUPS)])
              for j in range(EXPERTS_PER_GROUP)]
    awithin = [_pick(gidx, [aff_rows[4 * g + j] for g in range(N_GROUPS)])
               for j in range(EXPERTS_PER_GROUP)]
    _, i0 = _first_max(within)
    masked = [jnp.where(i0 == j, -jnp.inf, within[j]) for j in range(EXPERTS_PER_GROUP)]
    _, i1 = _first_max(masked)
    a0, a1 = _pick(i0, awithin), _pick(i1, awithin)
    denom = a0 + a1
    w0, w1 = a0 / denom, a1 / denom

    swap = i1 < i0
    lo = jnp.where(swap, i1, i0)
    hi = jnp.where(swap, i0, i1)
    w_lo = jnp.where(swap, w1, w0)
    w_hi = jnp.where(swap, w0, w1)
    pair = _pick(lo, PAIR_BASE) + (hi - lo - 1)
    bucket = gidx * PAIRS_PER_GROUP + pair

    biota = lax.broadcasted_iota(jnp.int32, (BUCKET_ROWS, t), 0)
    oh = jnp.where(biota == bucket, 1.0, 0.0)
    row = lax.broadcasted_iota(jnp.int32, (t, t), 0)
    col = lax.broadcasted_iota(jnp.int32, (t, t), 1)
    before = jnp.where(row < col, 1.0, 0.0).astype(BF16)
    tot = base_ref[:, 0:1] + jnp.dot(oh.astype(BF16), before, preferred_element_type=F32)
    rank = jnp.sum(oh * tot, axis=0, keepdims=True)
    slot = lax.broadcasted_iota(jnp.int32, (2, t), 0)
    br_ref[...] = jnp.where(slot == 0, bucket, rank.astype(jnp.int32))
    newbase = base_ref[...] + live * jnp.sum(oh, axis=1, keepdims=True)
    base_ref[...] = newbase
    cnt_ref[...] = newbase

    wrow = lax.broadcasted_iota(jnp.int32, (LANES, t), 0)
    gw = jnp.where(wrow == 0, w_lo, jnp.where(wrow == 1, w_hi, 0.0))
    gwt_ref[...] = jnp.transpose(gw)

    yb = jnp.dot(ob_ref[...], wob_ref[...], preferred_element_type=F32)
    mixin = sga_ref[...].astype(F32) * ya_ref[...].astype(F32) + sgb_ref[...].astype(F32) * yb
    mix = jnp.dot(mixin.astype(BF16), wo_ref[...], preferred_element_type=F32)
    x1 = _layer_norm(ALPHA * x_ref[...] + mix, g_ref[...], b_ref[...])
    x1_ref[...] = x1
    x_hi = x1.astype(BF16)
    x_lo = (x1 - x_hi.astype(F32)).astype(BF16)
    r_hi = jnp.dot(x_hi, rw2_ref[...], preferred_element_type=F32)
    r_lo = jnp.dot(x_lo, rw2_ref[...], preferred_element_type=F32)
    logit_ref[...] = ((r_hi[:, :LANES] + r_hi[:, LANES:])
                      + (r_lo[:, :LANES] + r_lo[:, LANES:]))


def _merge(ob, ya, gates, xf, wob, wo, g, b, rw2, rb):
    m = xf.shape[0]
    t = MERGE_T
    last = m // t - 1

    def cur(col):
        return lambda i: (jnp.minimum(i, last), col)

    return pl.pallas_call(
        _merge_kernel,
        grid=(m // t + 1,),
        in_specs=[
            pl.BlockSpec((t, B_WIDTH), cur(0)),
            pl.BlockSpec((t, D_MODEL), cur(0)),
            pl.BlockSpec((t, D_MODEL), cur(0)),
            pl.BlockSpec((t, D_MODEL), cur(1)),
            pl.BlockSpec((t, D_MODEL), cur(0)),
            _const_spec((B_WIDTH, D_MODEL)),
            _const_spec((D_MODEL, D_MODEL)),
            _const_spec((1, D_MODEL)),
            _const_spec((1, D_MODEL)),
            _const_spec((D_MODEL, 2 * LANES)),
            _const_spec((N_EXPERTS, 1)),
        ],
        out_specs=[
            pl.BlockSpec((t, D_MODEL), cur(0)),
            pl.BlockSpec((2, t), lambda i: (0, jnp.maximum(i - 1, 0))),
            pl.BlockSpec((t, LANES), lambda i: (jnp.maximum(i - 1, 0), 0)),
            pl.BlockSpec((BUCKET_ROWS, LANES), lambda i: (0, 0)),
        ],
        out_shape=[
            jax.ShapeDtypeStruct((m, D_MODEL), F32),
            jax.ShapeDtypeStruct((2, m), jnp.int32),
            jax.ShapeDtypeStruct((m, LANES), F32),
            jax.ShapeDtypeStruct((BUCKET_ROWS, LANES), F32),
        ],
        scratch_shapes=[pltpu.VMEM((BUCKET_ROWS, LANES), F32), pltpu.VMEM((t, LANES), F32)],
        compiler_params=pltpu.CompilerParams(
            dimension_semantics=("arbitrary",), vmem_limit_bytes=VMEM_LIMIT),
        name="merge",
    )(ob, ya, gates, gates, xf, wob, wo, g, b, rw2, rb)


def _dispatch_row_copy(xext_ref, xd_ref, sem, src_row, dst_row):
    return pltpu.make_async_copy(xext_ref.at[pl.ds(src_row, 1), :], xd_ref.at[pl.ds(dst_row, 1), :], sem)


def _zero_fill_copy(zero_ref, xd_ref, sem, dst_row):
    rows = zero_ref.shape[0]
    return pltpu.make_async_copy(zero_ref, xd_ref.at[pl.ds(pl.multiple_of(dst_row, SUBLANES), rows), :], sem)


def _ple_dispatch_kernel(dest_ref, zstart_ref, nv_ref, x1_ref, p_ref, gwt_ref, wpg_ref, wpp_ref,
                         base_ref, xd_ref, xext_ref, zero_ref, sems, zsem):
    t = x1_ref.shape[0]
    first = pl.program_id(0) * t

    @pl.when(pl.program_id(0) == 0)
    def _():
        zero_ref[...] = jnp.zeros_like(zero_ref)
        for k in range(N_BUCKETS):
            _zero_fill_copy(zero_ref, xd_ref, zsem, zstart_ref[k]).start()
        for k in range(N_BUCKETS):
            _zero_fill_copy(zero_ref, xd_ref, zsem, zstart_ref[k]).wait()
        rows = zero_ref.shape[0]
        total_blocks = xd_ref.shape[0] // rows

        def fill(b, c):
            _zero_fill_copy(zero_ref, xd_ref, zsem, b * rows).start()
            return c

        def fill_done(b, c):
            _zero_fill_copy(zero_ref, xd_ref, zsem, b * rows).wait()
            return c

        lax.fori_loop(nv_ref[0], total_blocks, fill, 0)
        lax.fori_loop(nv_ref[0], total_blocks, fill_done, 0)

    step = pl.program_id(0)
    slot = step % 2
    x1 = x1_ref[...]
    xext_ref[slot, :, :D_MODEL] = x1
    xext_ref[slot, :, D_MODEL:] = gwt_ref[...]

    for r in range(t):
        _dispatch_row_copy(xext_ref.at[slot], xd_ref, sems.at[slot], r,
                           dest_ref[first + r]).start(priority=r % DMA_PRIORITIES)

    gate = jax.nn.sigmoid(jnp.dot(x1.astype(BF16), wpg_ref[...], preferred_element_type=F32))
    proj = jnp.dot(p_ref[...].astype(BF16), wpp_ref[...], preferred_element_type=F32)
    base_ref[...] = ALPHA * x1 + gate * proj

    def drain(s):
        for r in range(t):
            _dispatch_row_copy(xext_ref.at[s], xd_ref, sems.at[s], 0, 0).wait()

    @pl.when(step > 0)
    def _():
        drain(1 - slot)

    @pl.when(step == pl.num_programs(0) - 1)
    def _():
        drain(slot)


def _ple_dispatch(dest, zstart, n_valid, x1, p, layer, gwt, wpg, wpp, xd_rows):
    m, d = x1.shape
    t = DISPATCH_T
    return pl.pallas_call(
        _ple_dispatch_kernel,
        grid_spec=pltpu.PrefetchScalarGridSpec(
            num_scalar_prefetch=3,
            grid=(m // t,),
            in_specs=[
                pl.BlockSpec((t, d), lambda i, *_: (i, 0)),
                pl.BlockSpec((None, t, PLE_DIM), lambda i, *_: (layer, i, 0)),
                pl.BlockSpec((t, LANES), lambda i, *_: (i, 0)),
                _const_spec((d, d)),
                _layer_spec((PLE_DIM, d), layer),
            ],
            out_specs=[
                pl.BlockSpec((t, d), lambda i, *_: (i, 0)),
                pl.BlockSpec(memory_space=pl.ANY),
            ],
            scratch_shapes=[
                pltpu.VMEM((2, t, XD_WIDTH), F32),
                pltpu.VMEM((EXPERT_TM, XD_WIDTH), F32),
                pltpu.SemaphoreType.DMA((2,)),
                pltpu.SemaphoreType.DMA(()),
            ],
        ),
        out_shape=[jax.ShapeDtypeStruct((m, d), F32), jax.ShapeDtypeStruct((xd_rows, XD_WIDTH), F32)],
        compiler_params=pltpu.CompilerParams(
            dimension_semantics=("arbitrary",), vmem_limit_bytes=VMEM_LIMIT),
        name="ple_dispatch",
    )(dest, zstart, n_valid, x1, p, gwt, wpg, wpp)


def _ffn(xb, wg_ref, wu_ref, wd_ref):
    g = jnp.dot(xb, wg_ref[...], preferred_element_type=F32)
    u = jnp.dot(xb, wu_ref[...], preferred_element_type=F32)
    h = (jax.nn.silu(g) * u).astype(BF16)
    return jnp.dot(h, wd_ref[...], preferred_element_type=F32)


def _expert_kernel(lo_ref, hi_ref, nv_ref, x_ref, wgl_ref, wul_ref, wdl_ref, wgh_ref, wuh_ref, wdh_ref,
                   y_ref):
    del lo_ref, hi_ref
    b = pl.program_id(0)

    @pl.when(b < nv_ref[0])
    def _():
        xb = x_ref[:, :D_MODEL].astype(BF16)
        w_lo = x_ref[:, D_MODEL:D_MODEL + 1]
        w_hi = x_ref[:, D_MODEL + 1:D_MODEL + 2]
        y_ref[...] = (w_lo * _ffn(xb, wgl_ref, wul_ref, wdl_ref)
                      + w_hi * _ffn(xb, wgh_ref, wuh_ref, wdh_ref))

    @pl.when(b >= nv_ref[0])
    def _():
        y_ref[...] = jnp.zeros_like(y_ref)


def _experts(blk_lo, blk_hi, n_valid, xd, wg, wu, wd, n_blocks):
    d = D_MODEL
    tm = EXPERT_TM

    def x_map(b, lo, hi, nv):
        return (jnp.minimum(b, jnp.maximum(nv[0] - 1, 0)), 0)

    def lo_map(b, lo, hi, nv):
        return (lo[b], 0, 0)

    def hi_map(b, lo, hi, nv):
        return (hi[b], 0, 0)

    return pl.pallas_call(
        _expert_kernel,
        grid_spec=pltpu.PrefetchScalarGridSpec(
            num_scalar_prefetch=3,
            grid=(n_blocks,),
            in_specs=[
                pl.BlockSpec((tm, XD_WIDTH), x_map),
                pl.BlockSpec((None, d, D_EXPERT), lo_map),
                pl.BlockSpec((None, d, D_EXPERT), lo_map),
                pl.BlockSpec((None, D_EXPERT, d), lo_map),
                pl.BlockSpec((None, d, D_EXPERT), hi_map),
                pl.BlockSpec((None, d, D_EXPERT), hi_map),
                pl.BlockSpec((None, D_EXPERT, d), hi_map),
            ],
            out_specs=pl.BlockSpec((tm, d), lambda b, lo, hi, nv: (b, 0)),
        ),
        out_shape=jax.ShapeDtypeStruct((n_blocks * tm, d), F32),
        compiler_params=pltpu.CompilerParams(
            dimension_semantics=("arbitrary",), vmem_limit_bytes=EXPERT_VMEM_LIMIT),
        name="experts",
    )(blk_lo, blk_hi, n_valid, xd, wg, wu, wd, wg, wu, wd)


def _combine_row_copy(y_ref, buf_ref, sems, slot, r, src_row):
    return pltpu.make_async_copy(y_ref.at[pl.ds(src_row, 1), :], buf_ref.at[slot, pl.ds(r, 1), :],
                                 sems.at[slot])


def _final_kernel(dest_ref, base_ref, g_ref, b_ref, y_ref, xo_ref, xob_ref, buf_ref, sems):
    t = base_ref.shape[0]
    i = pl.program_id(0)
    last = pl.num_programs(0) - 1

    def issue(tile, slot):
        for r in range(t):
            _combine_row_copy(y_ref, buf_ref, sems, slot, r,
                              dest_ref[tile * t + r]).start(priority=r % DMA_PRIORITIES)

    def drain(slot):
        for r in range(t):
            _combine_row_copy(y_ref, buf_ref, sems, slot, 0, 0).wait()

    @pl.when(i == 0)
    def _():
        issue(0, 0)
        issue(jnp.minimum(1, last), 1)

    slot = i % FINAL_SLOTS
    drain(slot)
    xo_ref[...] = base_ref[...] + buf_ref[slot]
    issue(jnp.minimum(i + 2, last), (i + 2) % FINAL_SLOTS)
    x2 = _layer_norm(xo_ref[...], g_ref[...], b_ref[...])
    xo_ref[...] = x2
    xob_ref[...] = x2.astype(BF16)

    @pl.when(i == last)
    def _():
        drain((i + 1) % FINAL_SLOTS)
        drain((i + 2) % FINAL_SLOTS)


def _final(dest, base, g, b, y):
    m, d = base.shape
    t = FINAL_T
    return pl.pallas_call(
        _final_kernel,
        grid_spec=pltpu.PrefetchScalarGridSpec(
            num_scalar_prefetch=1,
            grid=(m // t,),
            in_specs=[
                pl.BlockSpec((t, d), lambda i, dest: (i, 0)),
                pl.BlockSpec((1, d), lambda i, dest: (0, 0), pipeline_mode=pl.Buffered(1)),
                pl.BlockSpec((1, d), lambda i, dest: (0, 0), pipeline_mode=pl.Buffered(1)),
                pl.BlockSpec(memory_space=pl.ANY),
            ],
            out_specs=[
                pl.BlockSpec((t, d), lambda i, dest: (i, 0)),
                pl.BlockSpec((t, d), lambda i, dest: (i, 0)),
            ],
            scratch_shapes=[pltpu.VMEM((FINAL_SLOTS, t, d), F32), pltpu.SemaphoreType.DMA((FINAL_SLOTS,))],
        ),
        out_shape=[jax.ShapeDtypeStruct((m, d), F32), jax.ShapeDtypeStruct((m, d), BF16)],
        compiler_params=pltpu.CompilerParams(
            dimension_semantics=("arbitrary",), vmem_limit_bytes=VMEM_LIMIT),
        name="final",
    )(dest, base, g, b, y)


def kernel(x, p, w_in, gmlp_ln_g, gmlp_ln_b, gmlp_ws, gmlp_bs, w_out_a, w_out_b, w_o, ln1_g, ln1_b,
           router_w, router_bias, exp_w_gate, exp_w_up, exp_w_down, ple_w_gate, ple_w_proj,
           ln2_g, ln2_b):
    batch, seq, d = x.shape
    m = batch * seq
    tm = EXPERT_TM
    n_blocks = m // tm + N_BUCKETS
    xd_rows = (n_blocks + 1) * tm

    xf = x.reshape(m, d)
    xb = xf.astype(BF16)
    p_flat = p.reshape(DEPTH, m, PLE_DIM)
    rw = jnp.pad(router_w.astype(F32), ((0, 0), (0, LANES - N_EXPERTS)))
    rwh = rw.astype(BF16)
    rw2 = jnp.concatenate([rwh, (rw - rwh.astype(F32)).astype(BF16)], axis=1)
    rb = router_bias.astype(F32).reshape(N_EXPERTS, 1)
    bucket_ids = jnp.arange(N_BUCKETS, dtype=jnp.int32)
    block_ids = jnp.arange(n_blocks, dtype=jnp.int32)
    wpp_b = ple_w_proj.astype(BF16)
    wg_rows = exp_w_gate.reshape(DEPTH, N_EXPERTS * D_MODEL, D_EXPERT)
    wu_rows = exp_w_up.reshape(DEPTH, N_EXPERTS * D_MODEL, D_EXPERT)
    wd_rows = exp_w_down.reshape(DEPTH, N_EXPERTS * D_EXPERT, D_MODEL)

    for i in range(DEPTH):
        ug, wd_b = _inproj(xb, w_in, i, 0, 2 * A_WIDTH, jax.nn.gelu, cast=(wd_rows,))
        qkv = _inproj(xb, w_in, i, 2 * A_WIDTH, 3 * B_WIDTH, lambda a: a,
                      first_tile_scale=B_HEAD_DIM ** -0.5 * LOG2E)
        gates, wg_b, wu_b, w_out_a_b, w_out_b_b, w_o_b, wpg_b = _inproj(
            xb, w_in, i, 2 * A_WIDTH + 3 * B_WIDTH, 2 * D_MODEL, jax.nn.sigmoid,
            cast=(wg_rows, wu_rows, w_out_a, w_out_b, w_o, ple_w_gate))
        wg_b = wg_b.reshape(N_EXPERTS, D_MODEL, D_EXPERT)
        wu_b = wu_b.reshape(N_EXPERTS, D_MODEL, D_EXPERT)
        wd_b = wd_b.reshape(N_EXPERTS, D_EXPERT, D_MODEL)

        bias = jnp.repeat(jnp.transpose(gmlp_bs[i]), A_HEAD_DIM, axis=1)
        ya = _gmlp(ug, gmlp_ln_g[i].reshape(1, A_WIDTH), gmlp_ln_b[i].reshape(1, A_WIDTH),
                   gmlp_ws, i, bias, w_out_a_b)
        ob = _attention(qkv, batch, seq)

        x1, br, gwt, cnt = _merge(
            ob, ya, gates, xf, w_out_b_b, w_o_b,
            ln1_g[i].reshape(1, d), ln1_b[i].reshape(1, d), rw2, rb)

        counts = cnt[:N_BUCKETS, 0].astype(jnp.int32)
        nblk = jnp.maximum((counts + tm - 1) // tm, 1)
        blk_end = jnp.cumsum(nblk)
        pstart = (blk_end - nblk) * tm
        dest = jnp.sum(jnp.where(br[0][None, :] == bucket_ids[:, None], pstart[:, None], 0), axis=0) + br[1]
        zstart = pstart + counts // SUBLANES * SUBLANES
        blk_bucket = jnp.minimum(
            jnp.sum((block_ids[:, None] >= blk_end[None, :]).astype(jnp.int32), axis=1), N_BUCKETS - 1)
        blk_lo = jnp.asarray(BUCKET_LO)[blk_bucket]
        blk_hi = jnp.asarray(BUCKET_HI)[blk_bucket]
        n_valid = blk_end[-1:].astype(jnp.int32)

        base, xd = _ple_dispatch(dest, zstart, n_valid, x1, p_flat, i, gwt, wpg_b, wpp_b, xd_rows)
        y = _experts(blk_lo, blk_hi, n_valid, xd, wg_b, wu_b, wd_b, n_blocks)
        xf, xb = _final(dest, base, ln2_g[i].reshape(1, d), ln2_b[i].reshape(1, d), y)

    return xf.reshape(batch, seq, d)
```

```python
import functools

import jax
import jax.numpy as jnp
import numpy as np
from jax import lax
from jax.experimental import pallas as pl
from jax.experimental.pallas import tpu as pltpu

D_MODEL = 2048
DEPTH = 4
A_HEADS = 8
A_HEAD_DIM = 128
A_WIDTH = A_HEADS * A_HEAD_DIM
CHUNK = 128
B_HEADS = 8
B_HEAD_DIM = 128
B_WIDTH = B_HEADS * B_HEAD_DIM
N_EXPERTS = 16
N_GROUPS = 4
EXPERTS_PER_GROUP = N_EXPERTS // N_GROUPS
D_EXPERT = 1024
PLE_DIM = 256
ALPHA = (2 * DEPTH) ** 0.25
LN_EPS = 1e-5
LOG2E = 1.4426950408889634

LANES = 128
SUBLANES = 8
VMEM_LIMIT = 56 * 1024 * 1024
EXPERT_VMEM_LIMIT = 62 * 1024 * 1024

F32 = jnp.float32
BF16 = jnp.bfloat16

PAIRS = [(lo, hi) for lo in range(EXPERTS_PER_GROUP) for hi in range(lo + 1, EXPERTS_PER_GROUP)]
PAIRS_PER_GROUP = len(PAIRS)
N_BUCKETS = N_GROUPS * PAIRS_PER_GROUP
BUCKET_ROWS = 32
BUCKET_LO = np.array([g * EXPERTS_PER_GROUP + lo for g in range(N_GROUPS) for lo, _ in PAIRS], np.int32)
BUCKET_HI = np.array([g * EXPERTS_PER_GROUP + hi for g in range(N_GROUPS) for _, hi in PAIRS], np.int32)
PAIR_BASE = [PAIRS.index((lo, lo + 1)) for lo in range(EXPERTS_PER_GROUP - 1)]
XD_WIDTH = D_MODEL + LANES

INPROJ_TM = 1024
INPROJ_TN = 1024
INPROJ_CHUNKS = 4
GMLP_T = 512
ATTN_T = 256
ATTN_HEADS_PER_STEP = 8
ATTN_ROW_PARTS = 1
MERGE_T = 256
DISPATCH_T = 256
EXPERT_TM = 256
FINAL_T = 256
FINAL_SLOTS = 3
DMA_PRIORITIES = 2


def _layer_norm(xf, gain, bias):
    mu = jnp.mean(xf, axis=-1, keepdims=True)
    xc = xf - mu
    var = jnp.mean(xc * xc, axis=-1, keepdims=True)
    return xc * lax.rsqrt(var + LN_EPS) * gain + bias


def _const_spec(shape):
    nd = len(shape)
    return pl.BlockSpec(shape, lambda *_: (0,) * nd, pipeline_mode=pl.Buffered(1))


def _layer_spec(shape, layer):
    nd = len(shape)
    return pl.BlockSpec((None,) + tuple(shape), lambda *_: (layer,) + (0,) * nd,
                        pipeline_mode=pl.Buffered(1))


def _inproj_kernel(x_ref, w_ref, *refs, act, first_tile_scale, n_cast):
    cast_src, o_ref, cast_dst, wb_ref = refs[:n_cast], refs[n_cast], refs[n_cast + 1:-1], refs[-1]

    @pl.when(pl.program_id(1) == 0)
    def _():
        wb_ref[...] = w_ref[...].astype(BF16)

    for src, dst in zip(cast_src, cast_dst):
        dst[...] = src[...].astype(BF16)

    rows = x_ref.shape[0] // INPROJ_CHUNKS
    for c in range(INPROJ_CHUNKS):
        rs = slice(c * rows, (c + 1) * rows)
        y = act(jnp.dot(x_ref[rs, :], wb_ref[...], preferred_element_type=F32))
        if first_tile_scale is not None:
            y = y * jnp.where(pl.program_id(0) == 0, first_tile_scale, 1.0)
        o_ref[rs, :] = y.astype(o_ref.dtype)


def _inproj(xb, w, layer, col0, ncols, act, first_tile_scale=None, cast=()):
    m, k = xb.shape
    tm, tn = INPROJ_TM, INPROJ_TN
    cb0 = col0 // tn
    n_i = m // tm
    steps = (ncols // tn) * n_i
    cast_in_specs, cast_out_specs, cast_shapes = [], [], []
    for arr in cast:
        _, rows, cols = arr.shape
        slab = rows // steps
        cast_in_specs.append(pl.BlockSpec((None, slab, cols), lambda j, i: (layer, j * n_i + i, 0)))
        cast_out_specs.append(pl.BlockSpec((slab, cols), lambda j, i: (j * n_i + i, 0)))
        cast_shapes.append(jax.ShapeDtypeStruct((rows, cols), BF16))
    out = pl.pallas_call(
        functools.partial(_inproj_kernel, act=act, first_tile_scale=first_tile_scale, n_cast=len(cast)),
        grid=(ncols // tn, n_i),
        in_specs=[
            pl.BlockSpec((tm, k), lambda j, i: (i, 0)),
            pl.BlockSpec((None, k, tn), lambda j, i: (layer, 0, cb0 + j), pipeline_mode=pl.Buffered(1)),
        ] + cast_in_specs,
        out_specs=[pl.BlockSpec((tm, tn), lambda j, i: (i, j))] + cast_out_specs,
        out_shape=[jax.ShapeDtypeStruct((m, ncols), BF16)] + cast_shapes,
        scratch_shapes=[pltpu.VMEM((k, tn), BF16)],
        compiler_params=pltpu.CompilerParams(
            dimension_semantics=("arbitrary", "arbitrary"), vmem_limit_bytes=VMEM_LIMIT),
        name="inproj",
    )(xb, w, *cast)
    return out if cast else out[0]


def _gmlp_kernel(u_ref, gv_ref, lng_ref, lnb_ref, ws_ref, bias_ref, wo_ref, o_ref, yin_ref):
    t = u_ref.shape[0]
    nc = t // CHUNK
    v = _layer_norm(gv_ref[...].astype(F32), lng_ref[...], lnb_ref[...]).astype(BF16)
    row = lax.broadcasted_iota(jnp.int32, (CHUNK, CHUNK), 0)
    col = lax.broadcasted_iota(jnp.int32, (CHUNK, CHUNK), 1)
    tril = row >= col
    for h in range(A_HEADS):
        hs = slice(h * A_HEAD_DIM, (h + 1) * A_HEAD_DIM)
        wsm = jnp.where(tril, ws_ref[h], 0.0).astype(BF16)
        vh = jnp.concatenate([v[c * CHUNK:(c + 1) * CHUNK, hs] for c in range(nc)], axis=1)
        f = jnp.dot(wsm, vh, preferred_element_type=F32)
        bh = bias_ref[:, hs]
        for c in range(nc):
            rs = slice(c * CHUNK, (c + 1) * CHUNK)
            fc = f[:, c * A_HEAD_DIM:(c + 1) * A_HEAD_DIM] + bh
            yin_ref[rs, hs] = (u_ref[rs, hs].astype(F32) * fc).astype(BF16)
    o_ref[...] = jnp.dot(yin_ref[...], wo_ref[...], preferred_element_type=F32).astype(o_ref.dtype)


def _gmlp(ug, lng, lnb, ws, layer, bias, wo):
    m = ug.shape[0]
    t = GMLP_T
    return pl.pallas_call(
        _gmlp_kernel,
        grid=(m // t,),
        in_specs=[
            pl.BlockSpec((t, A_WIDTH), lambda i: (i, 0)),
            pl.BlockSpec((t, A_WIDTH), lambda i: (i, 1)),
            _const_spec((1, A_WIDTH)),
            _const_spec((1, A_WIDTH)),
            _layer_spec((A_HEADS, CHUNK, CHUNK), layer),
            _const_spec((CHUNK, A_WIDTH)),
            _const_spec((A_WIDTH, D_MODEL)),
        ],
        out_specs=pl.BlockSpec((t, D_MODEL), lambda i: (i, 0)),
        out_shape=jax.ShapeDtypeStruct((m, D_MODEL), BF16),
        scratch_shapes=[pltpu.VMEM((t, A_WIDTH), BF16)],
        compiler_params=pltpu.CompilerParams(
            dimension_semantics=("parallel",), vmem_limit_bytes=VMEM_LIMIT),
        name="gmlp",
    )(ug, ug, lng, lnb, ws, bias, wo)


def _attn_kernel(q_ref, k_ref, v_ref, o_ref):
    t = q_ref.shape[0]
    nh = q_ref.shape[1] // B_HEAD_DIM
    heads = range(nh)
    i = pl.program_id(2)
    row = lax.broadcasted_iota(jnp.int32, (t, t), 0)
    col = lax.broadcasted_iota(jnp.int32, (t, t), 1)
    neg_upper = jnp.where(row > col, -1.0, 0.0).astype(BF16)
    causal = col < row
    hs = [slice(h * B_HEAD_DIM, (h + 1) * B_HEAD_DIM) for h in heads]
    rp = t // ATTN_ROW_PARTS
    units = [(h, slice(r * rp, (r + 1) * rp)) for h in heads for r in range(ATTN_ROW_PARTS)]
    qs = [q_ref[rs, hs[h]] for h, rs in units]
    masks = [causal[rs, :] for _, rs in units]

    def tiles(off, state, diag):
        z = [lax.dot_general(qs[u], k_ref[pl.ds(off, t), hs[h]], (((1,), (1,)), ((), ())),
                             preferred_element_type=F32) for u, (h, _) in enumerate(units)]
        sp, logsig, packed = [], [], []
        for u in range(len(units)):
            neg_abs = pltpu.bitcast(pltpu.bitcast(z[u], jnp.uint32) | jnp.uint32(0x80000000), F32)
            s = jnp.maximum(z[u], 0.0) + jnp.log(1.0 + jnp.exp2(neg_abs)) * LOG2E
            logsig.append(z[u] - s)
            if diag:
                s = jnp.where(masks[u], s, 0.0)
            sp.append(s)
            packed.append(s.astype(BF16))
        suffix = [jnp.dot(packed[u], neg_upper, preferred_element_type=F32) for u in range(len(units))]
        a = []
        for u in range(len(units)):
            au = jnp.exp2(logsig[u] + suffix[u] + state[u][0])
            if diag:
                au = jnp.where(masks[u], au, 0.0)
            a.append(au.astype(BF16))
        return tuple(
            (state[u][0] - jnp.sum(sp[u], axis=1, keepdims=True),
             state[u][1] + jnp.dot(a[u], v_ref[pl.ds(off, t), hs[h]], preferred_element_type=F32))
            for u, (h, _) in enumerate(units))

    init = tuple((jnp.zeros((rp, 1), F32), jnp.zeros((rp, B_HEAD_DIM), F32)) for _ in units)
    state = tiles(pl.multiple_of(i * t, t), init, True)

    def body(jj, st):
        return tiles(pl.multiple_of((i - 1 - jj) * t, t), st, False)

    state = lax.fori_loop(0, i, body, state)
    for u, (h, rs) in enumerate(units):
        o_ref[rs, hs[h]] = state[u][1].astype(o_ref.dtype)


def _attention(qkv, batch, seq):
    m = qkv.shape[0]
    t = ATTN_T
    nq = seq // t
    w = ATTN_HEADS_PER_STEP * B_HEAD_DIM
    hb = B_WIDTH // w
    return pl.pallas_call(
        _attn_kernel,
        grid=(batch, hb, nq),
        in_specs=[
            pl.BlockSpec((t, w), lambda b, h, i: (b * nq + i, h)),
            pl.BlockSpec((seq, w), lambda b, h, i: (b, hb + h)),
            pl.BlockSpec((seq, w), lambda b, h, i: (b, 2 * hb + h)),
        ],
        out_specs=pl.BlockSpec((t, w), lambda b, h, i: (b * nq + i, h)),
        out_shape=jax.ShapeDtypeStruct((m, B_WIDTH), BF16),
        compiler_params=pltpu.CompilerParams(
            dimension_semantics=("parallel", "parallel", "parallel"), vmem_limit_bytes=VMEM_LIMIT),
        name="attn",
    )(qkv, qkv, qkv)


def _first_max(vals):
    best, idx = vals[0], jnp.zeros(vals[0].shape, jnp.int32)
    for j in range(1, len(vals)):
        better = vals[j] > best
        best = jnp.where(better, vals[j], best)
        idx = jnp.where(better, j, idx)
    return best, idx


def _pick(idx, vals):
    out = vals[0]
    for j in range(1, len(vals)):
        out = jnp.where(idx == j, vals[j], out)
    return out


def _merge_kernel(ob_ref, ya_ref, sga_ref, sgb_ref, x_ref, wob_ref, wo_ref, g_ref, b_ref,
                  rw2_ref, rb_ref, x1_ref, br_ref, gwt_ref, cnt_ref, base_ref, logit_ref):
    t = x_ref.shape[0]
    step = pl.program_id(0)

    @pl.when(step == 0)
    def _():
        base_ref[...] = jnp.zeros_like(base_ref)
        logit_ref[...] = jnp.zeros_like(logit_ref)

    live = jnp.where(step > 0, 1.0, 0.0)

    lt = jnp.transpose(logit_ref[...])[:N_EXPERTS, :]
    aff = jax.nn.sigmoid(lt)
    sel = aff + rb_ref[...]
    sel_rows = [sel[e:e + 1, :] for e in range(N_EXPERTS)]
    aff_rows = [aff[e:e + 1, :] for e in range(N_EXPERTS)]
    gscore = []
    for g in range(N_GROUPS):
        s0, s1, s2, s3 = sel_rows[4 * g:4 * g + 4]
        hi1, lo1 = jnp.maximum(s0, s1), jnp.minimum(s0, s1)
        hi2, lo2 = jnp.maximum(s2, s3), jnp.minimum(s2, s3)
        top1 = jnp.maximum(hi1, hi2)
        top2 = jnp.maximum(jnp.minimum(hi1, hi2), jnp.maximum(lo1, lo2))
        gscore.append(top1 + top2)
    _, gidx = _first_max(gscore)
    within = [_pick(gidx, [sel_rows[4 * g + j] for g in range(N_GROUPS)])
              for j in range(EXPERTS_PER_GROUP)]
    awithin = [_pick(gidx, [aff_rows[4 * g + j] for g in range(N_GROUPS)])
               for j in range(EXPERTS_PER_GROUP)]
    _, i0 = _first_max(within)
    masked = [jnp.where(i0 == j, -jnp.inf, within[j]) for j in range(EXPERTS_PER_GROUP)]
    _, i1 = _first_max(masked)
    a0, a1 = _pick(i0, awithin), _pick(i1, awithin)
    denom = a0 + a1
    w0, w1 = a0 / denom, a1 / denom

    swap = i1 < i0
    lo = jnp.where(swap, i1, i0)
    hi = jnp.where(swap, i0, i1)
    w_lo = jnp.where(swap, w1, w0)
    w_hi = jnp.where(swap, w0, w1)
    pair = _pick(lo, PAIR_BASE) + (hi - lo - 1)
    bucket = gidx * PAIRS_PER_GROUP + pair

    biota = lax.broadcasted_iota(jnp.int32, (BUCKET_ROWS, t), 0)
    oh = jnp.where(biota == bucket, 1.0, 0.0)
    row = lax.broadcasted_iota(jnp.int32, (t, t), 0)
    col = lax.broadcasted_iota(jnp.int32, (t, t), 1)
    before = jnp.where(row < col, 1.0, 0.0).astype(BF16)
    tot = base_ref[:, 0:1] + jnp.dot(oh.astype(BF16), before, preferred_element_type=F32)
    rank = jnp.sum(oh * tot, axis=0, keepdims=True)
    slot = lax.broadcasted_iota(jnp.int32, (2, t), 0)
    br_ref[...] = jnp.where(slot == 0, bucket, rank.astype(jnp.int32))
    newbase = base_ref[...] + live * jnp.sum(oh, axis=1, keepdims=True)
    base_ref[...] = newbase
    cnt_ref[...] = newbase

    wrow = lax.broadcasted_iota(jnp.int32, (LANES, t), 0)
    gw = jnp.where(wrow == 0, w_lo, jnp.where(wrow == 1, w_hi, 0.0))
    gwt_ref[...] = jnp.transpose(gw)

    yb = jnp.dot(ob_ref[...], wob_ref[...], preferred_element_type=F32)
    mixin = sga_ref[...].astype(F32) * ya_ref[...].astype(F32) + sgb_ref[...].astype(F32) * yb
    mix = jnp.dot(mixin.astype(BF16), wo_ref[...], preferred_element_type=F32)
    x1 = _layer_norm(ALPHA * x_ref[...] + mix, g_ref[...], b_ref[...])
    x1_ref[...] = x1
    x_hi = x1.astype(BF16)
    x_lo = (x1 - x_hi.astype(F32)).astype(BF16)
    r_hi = jnp.dot(x_hi, rw2_ref[...], preferred_element_type=F32)
    r_lo = jnp.dot(x_lo, rw2_ref[...], preferred_element_type=F32)
    logit_ref[...] = ((r_hi[:, :LANES] + r_hi[:, LANES:])
                      + (r_lo[:, :LANES] + r_lo[:, LANES:]))


def _merge(ob, ya, gates, xf, wob, wo, g, b, rw2, rb):
    m = xf.shape[0]
    t = MERGE_T
    last = m // t - 1

    def cur(col):
        return lambda i: (jnp.minimum(i, last), col)

    return pl.pallas_call(
        _merge_kernel,
        grid=(m // t + 1,),
        in_specs=[
            pl.BlockSpec((t, B_WIDTH), cur(0)),
            pl.BlockSpec((t, D_MODEL), cur(0)),
            pl.BlockSpec((t, D_MODEL), cur(0)),
            pl.BlockSpec((t, D_MODEL), cur(1)),
            pl.BlockSpec((t, D_MODEL), cur(0)),
            _const_spec((B_WIDTH, D_MODEL)),
            _const_spec((D_MODEL, D_MODEL)),
            _const_spec((1, D_MODEL)),
            _const_spec((1, D_MODEL)),
            _const_spec((D_MODEL, 2 * LANES)),
            _const_spec((N_EXPERTS, 1)),
        ],
        out_specs=[
            pl.BlockSpec((t, D_MODEL), cur(0)),
            pl.BlockSpec((2, t), lambda i: (0, jnp.maximum(i - 1, 0))),
            pl.BlockSpec((t, LANES), lambda i: (jnp.maximum(i - 1, 0), 0)),
            pl.BlockSpec((BUCKET_ROWS, LANES), lambda i: (0, 0)),
        ],
        out_shape=[
            jax.ShapeDtypeStruct((m, D_MODEL), F32),
            jax.ShapeDtypeStruct((2, m), jnp.int32),
            jax.ShapeDtypeStruct((m, LANES), F32),
            jax.ShapeDtypeStruct((BUCKET_ROWS, LANES), F32),
        ],
        scratch_shapes=[pltpu.VMEM((BUCKET_ROWS, LANES), F32), pltpu.VMEM((t, LANES), F32)],
        compiler_params=pltpu.CompilerParams(
            dimension_semantics=("arbitrary",), vmem_limit_bytes=VMEM_LIMIT),
        name="merge",
    )(ob, ya, gates, gates, xf, wob, wo, g, b, rw2, rb)


def _dispatch_row_copy(xext_ref, xd_ref, sem, src_row, dst_row):
    return pltpu.make_async_copy(xext_ref.at[pl.ds(src_row, 1), :], xd_ref.at[pl.ds(dst_row, 1), :], sem)


def _zero_fill_copy(zero_ref, xd_ref, sem, dst_row):
    rows = zero_ref.shape[0]
    return pltpu.make_async_copy(zero_ref, xd_ref.at[pl.ds(pl.multiple_of(dst_row, SUBLANES), rows), :], sem)


def _ple_dispatch_kernel(dest_ref, zstart_ref, nv_ref, x1_ref, p_ref, gwt_ref, wpg_ref, wpp_ref,
                         base_ref, xd_ref, xext_ref, zero_ref, sems, zsem):
    t = x1_ref.shape[0]
    first = pl.program_id(0) * t

    @pl.when(pl.program_id(0) == 0)
    def _():
        zero_ref[...] = jnp.zeros_like(zero_ref)
        for k in range(N_BUCKETS):
            _zero_fill_copy(zero_ref, xd_ref, zsem, zstart_ref[k]).start()
        for k in range(N_BUCKETS):
            _zero_fill_copy(zero_ref, xd_ref, zsem, zstart_ref[k]).wait()
        rows = zero_ref.shape[0]
        total_blocks = xd_ref.shape[0] // rows

        def fill(b, c):
            _zero_fill_copy(zero_ref, xd_ref, zsem, b * rows).start()
            return c

        def fill_done(b, c):
            _zero_fill_copy(zero_ref, xd_ref, zsem, b * rows).wait()
            return c

        lax.fori_loop(nv_ref[0], total_blocks, fill, 0)
        lax.fori_loop(nv_ref[0], total_blocks, fill_done, 0)

    step = pl.program_id(0)
    slot = step % 2
    x1 = x1_ref[...]
    xext_ref[slot, :, :D_MODEL] = x1
    xext_ref[slot, :, D_MODEL:] = gwt_ref[...]

    for r in range(t):
        _dispatch_row_copy(xext_ref.at[slot], xd_ref, sems.at[slot], r,
                           dest_ref[first + r]).start(priority=r % DMA_PRIORITIES)

    gate = jax.nn.sigmoid(jnp.dot(x1.astype(BF16), wpg_ref[...], preferred_element_type=F32))
    proj = jnp.dot(p_ref[...].astype(BF16), wpp_ref[...], preferred_element_type=F32)
    base_ref[...] = ALPHA * x1 + gate * proj

    def drain(s):
        for r in range(t):
            _dispatch_row_copy(xext_ref.at[s], xd_ref, sems.at[s], 0, 0).wait()

    @pl.when(step > 0)
    def _():
        drain(1 - slot)

    @pl.when(step == pl.num_programs(0) - 1)
    def _():
        drain(slot)


def _ple_dispatch(dest, zstart, n_valid, x1, p, layer, gwt, wpg, wpp, xd_rows):
    m, d = x1.shape
    t = DISPATCH_T
    return pl.pallas_call(
        _ple_dispatch_kernel,
        grid_spec=pltpu.PrefetchScalarGridSpec(
            num_scalar_prefetch=3,
            grid=(m // t,),
            in_specs=[
                pl.BlockSpec((t, d), lambda i, *_: (i, 0)),
                pl.BlockSpec((None, t, PLE_DIM), lambda i, *_: (layer, i, 0)),
                pl.BlockSpec((t, LANES), lambda i, *_: (i, 0)),
                _const_spec((d, d)),
                _layer_spec((PLE_DIM, d), layer),
            ],
            out_specs=[
                pl.BlockSpec((t, d), lambda i, *_: (i, 0)),
                pl.BlockSpec(memory_space=pl.ANY),
            ],
            scratch_shapes=[
                pltpu.VMEM((2, t, XD_WIDTH), F32),
                pltpu.VMEM((EXPERT_TM, XD_WIDTH), F32),
                pltpu.SemaphoreType.DMA((2,)),
                pltpu.SemaphoreType.DMA(()),
            ],
        ),
        out_shape=[jax.ShapeDtypeStruct((m, d), F32), jax.ShapeDtypeStruct((xd_rows, XD_WIDTH), F32)],
        compiler_params=pltpu.CompilerParams(
            dimension_semantics=("arbitrary",), vmem_limit_bytes=VMEM_LIMIT),
        name="ple_dispatch",
    )(dest, zstart, n_valid, x1, p, gwt, wpg, wpp)


def _ffn(xb, wg_ref, wu_ref, wd_ref):
    g = jnp.dot(xb, wg_ref[...], preferred_element_type=F32)
    u = jnp.dot(xb, wu_ref[...], preferred_element_type=F32)
    h = (jax.nn.silu(g) * u).astype(BF16)
    return jnp.dot(h, wd_ref[...], preferred_element_type=F32)


def _expert_kernel(lo_ref, hi_ref, nv_ref, x_ref, wgl_ref, wul_ref, wdl_ref, wgh_ref, wuh_ref, wdh_ref,
                   y_ref):
    del lo_ref, hi_ref
    b = pl.program_id(0)

    @pl.when(b < nv_ref[0])
    def _():
        xb = x_ref[:, :D_MODEL].astype(BF16)
        w_lo = x_ref[:, D_MODEL:D_MODEL + 1]
        w_hi = x_ref[:, D_MODEL + 1:D_MODEL + 2]
        y_ref[...] = (w_lo * _ffn(xb, wgl_ref, wul_ref, wdl_ref)
                      + w_hi * _ffn(xb, wgh_ref, wuh_ref, wdh_ref))

    @pl.when(b >= nv_ref[0])
    def _():
        y_ref[...] = jnp.zeros_like(y_ref)


def _experts(blk_lo, blk_hi, n_valid, xd, wg, wu, wd, n_blocks):
    d = D_MODEL
    tm = EXPERT_TM

    def x_map(b, lo, hi, nv):
        return (jnp.minimum(b, jnp.maximum(nv[0] - 1, 0)), 0)

    def lo_map(b, lo, hi, nv):
        return (lo[b], 0, 0)

    def hi_map(b, lo, hi, nv):
        return (hi[b], 0, 0)

    return pl.pallas_call(
        _expert_kernel,
        grid_spec=pltpu.PrefetchScalarGridSpec(
            num_scalar_prefetch=3,
            grid=(n_blocks,),
            in_specs=[
                pl.BlockSpec((tm, XD_WIDTH), x_map),
                pl.BlockSpec((None, d, D_EXPERT), lo_map),
                pl.BlockSpec((None, d, D_EXPERT), lo_map),
                pl.BlockSpec((None, D_EXPERT, d), lo_map),
                pl.BlockSpec((None, d, D_EXPERT), hi_map),
                pl.BlockSpec((None, d, D_EXPERT), hi_map),
                pl.BlockSpec((None, D_EXPERT, d), hi_map),
            ],
            out_specs=pl.BlockSpec((tm, d), lambda b, lo, hi, nv: (b, 0)),
        ),
        out_shape=jax.ShapeDtypeStruct((n_blocks * tm, d), F32),
        compiler_params=pltpu.CompilerParams(
            dimension_semantics=("arbitrary",), vmem_limit_bytes=EXPERT_VMEM_LIMIT),
        name="experts",
    )(blk_lo, blk_hi, n_valid, xd, wg, wu, wd, wg, wu, wd)


def _combine_row_copy(y_ref, buf_ref, sems, slot, r, src_row):
    return pltpu.make_async_copy(y_ref.at[pl.ds(src_row, 1), :], buf_ref.at[slot, pl.ds(r, 1), :],
                                 sems.at[slot])


def _final_kernel(dest_ref, base_ref, g_ref, b_ref, y_ref, xo_ref, xob_ref, buf_ref, sems):
    t = base_ref.shape[0]
    i = pl.program_id(0)
    last = pl.num_programs(0) - 1

    def issue(tile, slot):
        for r in range(t):
            _combine_row_copy(y_ref, buf_ref, sems, slot, r,
                              dest_ref[tile * t + r]).start(priority=r % DMA_PRIORITIES)

    def drain(slot):
        for r in range(t):
            _combine_row_copy(y_ref, buf_ref, sems, slot, 0, 0).wait()

    @pl.when(i == 0)
    def _():
        issue(0, 0)
        issue(jnp.minimum(1, last), 1)

    slot = i % FINAL_SLOTS
    drain(slot)
    xo_ref[...] = base_ref[...] + buf_ref[slot]
    issue(jnp.minimum(i + 2, last), (i + 2) % FINAL_SLOTS)
    x2 = _layer_norm(xo_ref[...], g_ref[...], b_ref[...])
    xo_ref[...] = x2
    xob_ref[...] = x2.astype(BF16)

    @pl.when(i == last)
    def _():
        drain((i + 1) % FINAL_SLOTS)
        drain((i + 2) % FINAL_SLOTS)


def _final(dest, base, g, b, y):
    m, d = base.shape
    t = FINAL_T
    return pl.pallas_call(
        _final_kernel,
        grid_spec=pltpu.PrefetchScalarGridSpec(
            num_scalar_prefetch=1,
            grid=(m // t,),
            in_specs=[
                pl.BlockSpec((t, d), lambda i, dest: (i, 0)),
                pl.BlockSpec((1, d), lambda i, dest: (0, 0), pipeline_mode=pl.Buffered(1)),
                pl.BlockSpec((1, d), lambda i, dest: (0, 0), pipeline_mode=pl.Buffered(1)),
                pl.BlockSpec(memory_space=pl.ANY),
            ],
            out_specs=[
                pl.BlockSpec((t, d), lambda i, dest: (i, 0)),
                pl.BlockSpec((t, d), lambda i, dest: (i, 0)),
            ],
            scratch_shapes=[pltpu.VMEM((FINAL_SLOTS, t, d), F32), pltpu.SemaphoreType.DMA((FINAL_SLOTS,))],
        ),
        out_shape=[jax.ShapeDtypeStruct((m, d), F32), jax.ShapeDtypeStruct((m, d), BF16)],
        compiler_params=pltpu.CompilerParams(
            dimension_semantics=("arbitrary",), vmem_limit_bytes=VMEM_LIMIT),
        name="final",
    )(dest, base, g, b, y)


def kernel(x, p, w_in, gmlp_ln_g, gmlp_ln_b, gmlp_ws, gmlp_bs, w_out_a, w_out_b, w_o, ln1_g, ln1_b,
           router_w, router_bias, exp_w_gate, exp_w_up, exp_w_down, ple_w_gate, ple_w_proj,
           ln2_g, ln2_b):
    batch, seq, d = x.shape
    m = batch * seq
    tm = EXPERT_TM
    n_blocks = m // tm + N_BUCKETS
    xd_rows = (n_blocks + 1) * tm

    xf = x.reshape(m, d)
    xb = xf.astype(BF16)
    p_flat = p.reshape(DEPTH, m, PLE_DIM)
    rw = jnp.pad(router_w.astype(F32), ((0, 0), (0, LANES - N_EXPERTS)))
    rwh = rw.astype(BF16)
    rw2 = jnp.concatenate([rwh, (rw - rwh.astype(F32)).astype(BF16)], axis=1)
    rb = router_bias.astype(F32).reshape(N_EXPERTS, 1)
    bucket_ids = jnp.arange(N_BUCKETS, dtype=jnp.int32)
    block_ids = jnp.arange(n_blocks, dtype=jnp.int32)
    wpp_b = ple_w_proj.astype(BF16)
    wg_rows = exp_w_gate.reshape(DEPTH, N_EXPERTS * D_MODEL, D_EXPERT)
    wu_rows = exp_w_up.reshape(DEPTH, N_EXPERTS * D_MODEL, D_EXPERT)
    wd_rows = exp_w_down.reshape(DEPTH, N_EXPERTS * D_EXPERT, D_MODEL)

    for i in range(DEPTH):
        ug, wd_b = _inproj(xb, w_in, i, 0, 2 * A_WIDTH, jax.nn.gelu, cast=(wd_rows,))
        qkv = _inproj(xb, w_in, i, 2 * A_WIDTH, 3 * B_WIDTH, lambda a: a,
                      first_tile_scale=B_HEAD_DIM ** -0.5 * LOG2E)
        gates, wg_b, wu_b, w_out_a_b, w_out_b_b, w_o_b, wpg_b = _inproj(
            xb, w_in, i, 2 * A_WIDTH + 3 * B_WIDTH, 2 * D_MODEL, jax.nn.sigmoid,
            cast=(wg_rows, wu_rows, w_out_a, w_out_b, w_o, ple_w_gate))
        wg_b = wg_b.reshape(N_EXPERTS, D_MODEL, D_EXPERT)
        wu_b = wu_b.reshape(N_EXPERTS, D_MODEL, D_EXPERT)
        wd_b = wd_b.reshape(N_EXPERTS, D_EXPERT, D_MODEL)

        bias = jnp.repeat(jnp.transpose(gmlp_bs[i]), A_HEAD_DIM, axis=1)
        ya = _gmlp(ug, gmlp_ln_g[i].reshape(1, A_WIDTH), gmlp_ln_b[i].reshape(1, A_WIDTH),
                   gmlp_ws, i, bias, w_out_a_b)
        ob = _attention(qkv, batch, seq)

        x1, br, gwt, cnt = _merge(
            ob, ya, gates, xf, w_out_b_b, w_o_b,
            ln1_g[i].reshape(1, d), ln1_b[i].reshape(1, d), rw2, rb)

        counts = cnt[:N_BUCKETS, 0].astype(jnp.int32)
        nblk = jnp.maximum((counts + tm - 1) // tm, 1)
        blk_end = jnp.cumsum(nblk)
        pstart = (blk_end - nblk) * tm
        dest = jnp.sum(jnp.where(br[0][None, :] == bucket_ids[:, None], pstart[:, None], 0), axis=0) + br[1]
        zstart = pstart + counts // SUBLANES * SUBLANES
        blk_bucket = jnp.minimum(
            jnp.sum((block_ids[:, None] >= blk_end[None, :]).astype(jnp.int32), axis=1), N_BUCKETS - 1)
        blk_lo = jnp.asarray(BUCKET_LO)[blk_bucket]
        blk_hi = jnp.asarray(BUCKET_HI)[blk_bucket]
        n_valid = blk_end[-1:].astype(jnp.int32)

        base, xd = _ple_dispatch(dest, zstart, n_valid, x1, p_flat, i, gwt, wpg_b, wpp_b, xd_rows)
        y = _experts(blk_lo, blk_hi, n_valid, xd, wg_b, wu_b, wd_b, n_blocks)
        xf, xb = _final(dest, base, ln2_g[i].reshape(1, d), ln2_b[i].reshape(1, d), y)

    return xf.reshape(batch, seq, d)
```

```python
import functools

import jax
import jax.numpy as jnp
import numpy as np
from jax import lax
from jax.experimental import pallas as pl
from jax.experimental.pallas import tpu as pltpu

D_MODEL = 2048
DEPTH = 4
A_HEADS = 8
A_HEAD_DIM = 128
A_WIDTH = A_HEADS * A_HEAD_DIM
CHUNK = 128
B_HEADS = 8
B_HEAD_DIM = 128
B_WIDTH = B_HEADS * B_HEAD_DIM
N_EXPERTS = 16
N_GROUPS = 4
EXPERTS_PER_GROUP = N_EXPERTS // N_GROUPS
D_EXPERT = 1024
PLE_DIM = 256
ALPHA = (2 * DEPTH) ** 0.25
LN_EPS = 1e-5
LOG2E = 1.4426950408889634

LANES = 128
SUBLANES = 8
VMEM_LIMIT = 56 * 1024 * 1024
EXPERT_VMEM_LIMIT = 62 * 1024 * 1024

F32 = jnp.float32
BF16 = jnp.bfloat16

PAIRS = [(lo, hi) for lo in range(EXPERTS_PER_GROUP) for hi in range(lo + 1, EXPERTS_PER_GROUP)]
PAIRS_PER_GROUP = len(PAIRS)
N_BUCKETS = N_GROUPS * PAIRS_PER_GROUP
BUCKET_ROWS = 32
BUCKET_LO = np.array([g * EXPERTS_PER_GROUP + lo for g in range(N_GROUPS) for lo, _ in PAIRS], np.int32)
BUCKET_HI = np.array([g * EXPERTS_PER_GROUP + hi for g in range(N_GROUPS) for _, hi in PAIRS], np.int32)
PAIR_BASE = [PAIRS.index((lo, lo + 1)) for lo in range(EXPERTS_PER_GROUP - 1)]
XD_WIDTH = D_MODEL + LANES

INPROJ_TM = 1024
INPROJ_TN = 1024
INPROJ_CHUNKS = 4
GMLP_T = 512
GMLP_PARTS = 2
ATTN_T = 256
ATTN_HEADS_PER_STEP = 8
ATTN_ROW_PARTS = 1
MERGE_T = 256
DISPATCH_T = 256
EXPERT_TM = 256
FINAL_T = 256
FINAL_SLOTS = 3
DMA_PRIORITIES = 2


def _layer_norm(xf, gain, bias):
    mu = jnp.mean(xf, axis=-1, keepdims=True)
    xc = xf - mu
    var = jnp.mean(xc * xc, axis=-1, keepdims=True)
    return xc * lax.rsqrt(var + LN_EPS) * gain + bias


def _const_spec(shape):
    nd = len(shape)
    return pl.BlockSpec(shape, lambda *_: (0,) * nd, pipeline_mode=pl.Buffered(1))


def _layer_spec(shape, layer):
    nd = len(shape)
    return pl.BlockSpec((None,) + tuple(shape), lambda *_: (layer,) + (0,) * nd,
                        pipeline_mode=pl.Buffered(1))


def _inproj_kernel(x_ref, w_ref, *refs, act, first_tile_scale, n_cast):
    cast_src, o_ref, cast_dst, wb_ref = refs[:n_cast], refs[n_cast], refs[n_cast + 1:-1], refs[-1]

    @pl.when(pl.program_id(1) == 0)
    def _():
        wb_ref[...] = w_ref[...].astype(BF16)

    for src, dst in zip(cast_src, cast_dst):
        dst[...] = src[...].astype(BF16)

    rows = x_ref.shape[0] // INPROJ_CHUNKS
    for c in range(INPROJ_CHUNKS):
        rs = slice(c * rows, (c + 1) * rows)
        y = act(jnp.dot(x_ref[rs, :], wb_ref[...], preferred_element_type=F32))
        if first_tile_scale is not None:
            y = y * jnp.where(pl.program_id(0) == 0, first_tile_scale, 1.0)
        o_ref[rs, :] = y.astype(o_ref.dtype)


def _inproj(xb, w, layer, col0, ncols, act, first_tile_scale=None, cast=()):
    m, k = xb.shape
    tm, tn = INPROJ_TM, INPROJ_TN
    cb0 = col0 // tn
    n_i = m // tm
    steps = (ncols // tn) * n_i
    cast_in_specs, cast_out_specs, cast_shapes = [], [], []
    for arr in cast:
        _, rows, cols = arr.shape
        slab = rows // steps
        cast_in_specs.append(pl.BlockSpec((None, slab, cols), lambda j, i: (layer, j * n_i + i, 0)))
        cast_out_specs.append(pl.BlockSpec((slab, cols), lambda j, i: (j * n_i + i, 0)))
        cast_shapes.append(jax.ShapeDtypeStruct((rows, cols), BF16))
    out = pl.pallas_call(
        functools.partial(_inproj_kernel, act=act, first_tile_scale=first_tile_scale, n_cast=len(cast)),
        grid=(ncols // tn, n_i),
        in_specs=[
            pl.BlockSpec((tm, k), lambda j, i: (i, 0)),
            pl.BlockSpec((None, k, tn), lambda j, i: (layer, 0, cb0 + j), pipeline_mode=pl.Buffered(1)),
        ] + cast_in_specs,
        out_specs=[pl.BlockSpec((tm, tn), lambda j, i: (i, j))] + cast_out_specs,
        out_shape=[jax.ShapeDtypeStruct((m, ncols), BF16)] + cast_shapes,
        scratch_shapes=[pltpu.VMEM((k, tn), BF16)],
        compiler_params=pltpu.CompilerParams(
            dimension_semantics=("arbitrary", "arbitrary"), vmem_limit_bytes=VMEM_LIMIT),
        name="inproj",
    )(xb, w, *cast)
    return out if cast else out[0]


def _gmlp_kernel(u_ref, gv_ref, lng_ref, lnb_ref, ws_ref, bias_ref, wo_ref, o_ref, yin_ref):
    t = u_ref.shape[0]
    row = lax.broadcasted_iota(jnp.int32, (CHUNK, CHUNK), 0)
    col = lax.broadcasted_iota(jnp.int32, (CHUNK, CHUNK), 1)
    tril = row >= col
    wsm = [jnp.where(tril, ws_ref[h], 0.0).astype(BF16) for h in range(A_HEADS)]
    tp = t // GMLP_PARTS
    nc = tp // CHUNK
    for part in range(GMLP_PARTS):
        r0 = part * tp
        v = _layer_norm(gv_ref[r0:r0 + tp, :].astype(F32), lng_ref[...], lnb_ref[...]).astype(BF16)
        for h in range(A_HEADS):
            hs = slice(h * A_HEAD_DIM, (h + 1) * A_HEAD_DIM)
            vh = jnp.concatenate([v[c * CHUNK:(c + 1) * CHUNK, hs] for c in range(nc)], axis=1)
            f = jnp.dot(wsm[h], vh, preferred_element_type=F32)
            bh = bias_ref[:, hs]
            for c in range(nc):
                rs = slice(r0 + c * CHUNK, r0 + (c + 1) * CHUNK)
                fc = f[:, c * A_HEAD_DIM:(c + 1) * A_HEAD_DIM] + bh
                yin_ref[rs, hs] = (u_ref[rs, hs].astype(F32) * fc).astype(BF16)
        o_ref[r0:r0 + tp, :] = jnp.dot(yin_ref[r0:r0 + tp, :], wo_ref[...],
                                       preferred_element_type=F32).astype(o_ref.dtype)


def _gmlp(ug, lng, lnb, ws, layer, bias, wo):
    m = ug.shape[0]
    t = GMLP_T
    return pl.pallas_call(
        _gmlp_kernel,
        grid=(m // t,),
        in_specs=[
            pl.BlockSpec((t, A_WIDTH), lambda i: (i, 0)),
            pl.BlockSpec((t, A_WIDTH), lambda i: (i, 1)),
            _const_spec((1, A_WIDTH)),
            _const_spec((1, A_WIDTH)),
            _layer_spec((A_HEADS, CHUNK, CHUNK), layer),
            _const_spec((CHUNK, A_WIDTH)),
            _const_spec((A_WIDTH, D_MODEL)),
        ],
        out_specs=pl.BlockSpec((t, D_MODEL), lambda i: (i, 0)),
        out_shape=jax.ShapeDtypeStruct((m, D_MODEL), BF16),
        scratch_shapes=[pltpu.VMEM((t, A_WIDTH), BF16)],
        compiler_params=pltpu.CompilerParams(
            dimension_semantics=("parallel",), vmem_limit_bytes=VMEM_LIMIT),
        name="gmlp",
    )(ug, ug, lng, lnb, ws, bias, wo)


def _attn_kernel(q_ref, k_ref, v_ref, o_ref):
    t = q_ref.shape[0]
    nh = q_ref.shape[1] // B_HEAD_DIM
    heads = range(nh)
    i = pl.program_id(2)
    row = lax.broadcasted_iota(jnp.int32, (t, t), 0)
    col = lax.broadcasted_iota(jnp.int32, (t, t), 1)
    neg_upper = jnp.where(row > col, -1.0, 0.0).astype(BF16)
    causal = col < row
    hs = [slice(h * B_HEAD_DIM, (h + 1) * B_HEAD_DIM) for h in heads]
    rp = t // ATTN_ROW_PARTS
    units = [(h, slice(r * rp, (r + 1) * rp)) for h in heads for r in range(ATTN_ROW_PARTS)]
    qs = [q_ref[rs, hs[h]] for h, rs in units]
    masks = [causal[rs, :] for _, rs in units]

    def tiles(off, state, diag):
        z = [lax.dot_general(qs[u], k_ref[pl.ds(off, t), hs[h]], (((1,), (1,)), ((), ())),
                             preferred_element_type=F32) for u, (h, _) in enumerate(units)]
        sp, logsig, packed = [], [], []
        for u in range(len(units)):
            neg_abs = pltpu.bitcast(pltpu.bitcast(z[u], jnp.uint32) | jnp.uint32(0x80000000), F32)
            s = jnp.maximum(z[u], 0.0) + jnp.log(1.0 + jnp.exp2(neg_abs)) * LOG2E
            logsig.append(z[u] - s)
            if diag:
                s = jnp.where(masks[u], s, 0.0)
            sp.append(s)
            packed.append(s.astype(BF16))
        suffix = [jnp.dot(packed[u], neg_upper, preferred_element_type=F32) for u in range(len(units))]
        a = []
        for u in range(len(units)):
            au = jnp.exp2(logsig[u] + suffix[u] + state[u][0])
            if diag:
                au = jnp.where(masks[u], au, 0.0)
            a.append(au.astype(BF16))
        return tuple(
            (state[u][0] - jnp.sum(sp[u], axis=1, keepdims=True),
             state[u][1] + jnp.dot(a[u], v_ref[pl.ds(off, t), hs[h]], preferred_element_type=F32))
            for u, (h, _) in enumerate(units))

    init = tuple((jnp.zeros((rp, 1), F32), jnp.zeros((rp, B_HEAD_DIM), F32)) for _ in units)
    state = tiles(pl.multiple_of(i * t, t), init, True)

    def body(jj, st):
        return tiles(pl.multiple_of((i - 1 - jj) * t, t), st, False)

    state = lax.fori_loop(0, i, body, state)
    for u, (h, rs) in enumerate(units):
        o_ref[rs, hs[h]] = state[u][1].astype(o_ref.dtype)


def _attention(qkv, batch, seq):
    m = qkv.shape[0]
    t = ATTN_T
    nq = seq // t
    w = ATTN_HEADS_PER_STEP * B_HEAD_DIM
    hb = B_WIDTH // w
    return pl.pallas_call(
        _attn_kernel,
        grid=(batch, hb, nq),
        in_specs=[
            pl.BlockSpec((t, w), lambda b, h, i: (b * nq + i, h)),
            pl.BlockSpec((seq, w), lambda b, h, i: (b, hb + h)),
            pl.BlockSpec((seq, w), lambda b, h, i: (b, 2 * hb + h)),
        ],
        out_specs=pl.BlockSpec((t, w), lambda b, h, i: (b * nq + i, h)),
        out_shape=jax.ShapeDtypeStruct((m, B_WIDTH), BF16),
        compiler_params=pltpu.CompilerParams(
            dimension_semantics=("parallel", "parallel", "parallel"), vmem_limit_bytes=VMEM_LIMIT),
        name="attn",
    )(qkv, qkv, qkv)


def _first_max(vals):
    best, idx = vals[0], jnp.zeros(vals[0].shape, jnp.int32)
    for j in range(1, len(vals)):
        better = vals[j] > best
        best = jnp.where(better, vals[j], best)
        idx = jnp.where(better, j, idx)
    return best, idx


def _pick(idx, vals):
    out = vals[0]
    for j in range(1, len(vals)):
        out = jnp.where(idx == j, vals[j], out)
    return out


def _merge_kernel(ob_ref, ya_ref, sga_ref, sgb_ref, x_ref, wob_ref, wo_ref, g_ref, b_ref,
                  rw2_ref, rb_ref, x1_ref, br_ref, gwt_ref, cnt_ref, base_ref, logit_ref):
    t = x_ref.shape[0]
    step = pl.program_id(0)

    @pl.when(step == 0)
    def _():
        base_ref[...] = jnp.zeros_like(base_ref)
        logit_ref[...] = jnp.zeros_like(logit_ref)

    live = jnp.where(step > 0, 1.0, 0.0)

    lt = jnp.transpose(logit_ref[...])[:N_EXPERTS, :]
    aff = jax.nn.sigmoid(lt)
    sel = aff + rb_ref[...]
    sel_rows = [sel[e:e + 1, :] for e in range(N_EXPERTS)]
    aff_rows = [aff[e:e + 1, :] for e in range(N_EXPERTS)]
    gscore = []
    for g in range(N_GROUPS):
        s0, s1, s2, s3 = sel_rows[4 * g:4 * g + 4]
        hi1, lo1 = jnp.maximum(s0, s1), jnp.minimum(s0, s1)
        hi2, lo2 = jnp.maximum(s2, s3), jnp.minimum(s2, s3)
        top1 = jnp.maximum(hi1, hi2)
        top2 = jnp.maximum(jnp.minimum(hi1, hi2), jnp.maximum(lo1, lo2))
        gscore.append(top1 + top2)
    _, gidx = _first_max(gscore)
    within = [_pick(gidx, [sel_rows[4 * g + j] for g in range(N_GROUPS)])
              for j in range(EXPERTS_PER_GROUP)]
    awithin = [_pick(gidx, [aff_rows[4 * g + j] for g in range(N_GROUPS)])
               for j in range(EXPERTS_PER_GROUP)]
    _, i0 = _first_max(within)
    masked = [jnp.where(i0 == j, -jnp.inf, within[j]) for j in range(EXPERTS_PER_GROUP)]
    _, i1 = _first_max(masked)
    a0, a1 = _pick(i0, awithin), _pick(i1, awithin)
    denom = a0 + a1
    w0, w1 = a0 / denom, a1 / denom

    swap = i1 < i0
    lo = jnp.where(swap, i1, i0)
    hi = jnp.where(swap, i0, i1)
    w_lo = jnp.where(swap, w1, w0)
    w_hi = jnp.where(swap, w0, w1)
    pair = _pick(lo, PAIR_BASE) + (hi - lo - 1)
    bucket = gidx * PAIRS_PER_GROUP + pair

    biota = lax.broadcasted_iota(jnp.int32, (BUCKET_ROWS, t), 0)
    oh = jnp.where(biota == bucket, 1.0, 0.0)
    row = lax.broadcasted_iota(jnp.int32, (t, t), 0)
    col = lax.broadcasted_iota(jnp.int32, (t, t), 1)
    before = jnp.where(row < col, 1.0, 0.0).astype(BF16)
    tot = base_ref[:, 0:1] + jnp.dot(oh.astype(BF16), before, preferred_element_type=F32)
    rank = jnp.sum(oh * tot, axis=0, keepdims=True)
    slot = lax.broadcasted_iota(jnp.int32, (2, t), 0)
    br_ref[...] = jnp.where(slot == 0, bucket, rank.astype(jnp.int32))
    newbase = base_ref[...] + live * jnp.sum(oh, axis=1, keepdims=True)
    base_ref[...] = newbase
    cnt_ref[...] = newbase

    wrow = lax.broadcasted_iota(jnp.int32, (LANES, t), 0)
    gw = jnp.where(wrow == 0, w_lo, jnp.where(wrow == 1, w_hi, 0.0))
    gwt_ref[...] = jnp.transpose(gw)

    yb = jnp.dot(ob_ref[...], wob_ref[...], preferred_element_type=F32)
    mixin = sga_ref[...].astype(F32) * ya_ref[...].astype(F32) + sgb_ref[...].astype(F32) * yb
    mix = jnp.dot(mixin.astype(BF16), wo_ref[...], preferred_element_type=F32)
    x1 = _layer_norm(ALPHA * x_ref[...] + mix, g_ref[...], b_ref[...])
    x1_ref[...] = x1
    x_hi = x1.astype(BF16)
    x_lo = (x1 - x_hi.astype(F32)).astype(BF16)
    r_hi = jnp.dot(x_hi, rw2_ref[...], preferred_element_type=F32)
    r_lo = jnp.dot(x_lo, rw2_ref[...], preferred_element_type=F32)
    logit_ref[...] = ((r_hi[:, :LANES] + r_hi[:, LANES:])
                      + (r_lo[:, :LANES] + r_lo[:, LANES:]))


def _merge(ob, ya, gates, xf, wob, wo, g, b, rw2, rb):
    m = xf.shape[0]
    t = MERGE_T
    last = m // t - 1

    def cur(col):
        return lambda i: (jnp.minimum(i, last), col)

    return pl.pallas_call(
        _merge_kernel,
        grid=(m // t + 1,),
        in_specs=[
            pl.BlockSpec((t, B_WIDTH), cur(0)),
            pl.BlockSpec((t, D_MODEL), cur(0)),
            pl.BlockSpec((t, D_MODEL), cur(0)),
            pl.BlockSpec((t, D_MODEL), cur(1)),
            pl.BlockSpec((t, D_MODEL), cur(0)),
            _const_spec((B_WIDTH, D_MODEL)),
            _const_spec((D_MODEL, D_MODEL)),
            _const_spec((1, D_MODEL)),
            _const_spec((1, D_MODEL)),
            _const_spec((D_MODEL, 2 * LANES)),
            _const_spec((N_EXPERTS, 1)),
        ],
        out_specs=[
            pl.BlockSpec((t, D_MODEL), cur(0)),
            pl.BlockSpec((2, t), lambda i: (0, jnp.maximum(i - 1, 0))),
            pl.BlockSpec((t, LANES), lambda i: (jnp.maximum(i - 1, 0), 0)),
            pl.BlockSpec((BUCKET_ROWS, LANES), lambda i: (0, 0)),
        ],
        out_shape=[
            jax.ShapeDtypeStruct((m, D_MODEL), F32),
            jax.ShapeDtypeStruct((2, m), jnp.int32),
            jax.ShapeDtypeStruct((m, LANES), F32),
            jax.ShapeDtypeStruct((BUCKET_ROWS, LANES), F32),
        ],
        scratch_shapes=[pltpu.VMEM((BUCKET_ROWS, LANES), F32), pltpu.VMEM((t, LANES), F32)],
        compiler_params=pltpu.CompilerParams(
            dimension_semantics=("arbitrary",), vmem_limit_bytes=VMEM_LIMIT),
        name="merge",
    )(ob, ya, gates, gates, xf, wob, wo, g, b, rw2, rb)


def _dispatch_row_copy(xext_ref, xd_ref, sem, src_row, dst_row):
    return pltpu.make_async_copy(xext_ref.at[pl.ds(src_row, 1), :], xd_ref.at[pl.ds(dst_row, 1), :], sem)


def _zero_fill_copy(zero_ref, xd_ref, sem, dst_row):
    rows = zero_ref.shape[0]
    return pltpu.make_async_copy(zero_ref, xd_ref.at[pl.ds(pl.multiple_of(dst_row, SUBLANES), rows), :], sem)


def _ple_dispatch_kernel(dest_ref, zstart_ref, nv_ref, x1_ref, p_ref, gwt_ref, wpg_ref, wpp_ref,
                         base_ref, xd_ref, xext_ref, zero_ref, sems, zsem):
    t = x1_ref.shape[0]
    first = pl.program_id(0) * t

    @pl.when(pl.program_id(0) == 0)
    def _():
        zero_ref[...] = jnp.zeros_like(zero_ref)
        for k in range(N_BUCKETS):
            _zero_fill_copy(zero_ref, xd_ref, zsem, zstart_ref[k]).start()
        for k in range(N_BUCKETS):
            _zero_fill_copy(zero_ref, xd_ref, zsem, zstart_ref[k]).wait()
        rows = zero_ref.shape[0]
        total_blocks = xd_ref.shape[0] // rows

        def fill(b, c):
            _zero_fill_copy(zero_ref, xd_ref, zsem, b * rows).start()
            return c

        def fill_done(b, c):
            _zero_fill_copy(zero_ref, xd_ref, zsem, b * rows).wait()
            return c

        lax.fori_loop(nv_ref[0], total_blocks, fill, 0)
        lax.fori_loop(nv_ref[0], total_blocks, fill_done, 0)

    step = pl.program_id(0)
    slot = step % 2
    x1 = x1_ref[...]
    xext_ref[slot, :, :D_MODEL] = x1
    xext_ref[slot, :, D_MODEL:] = gwt_ref[...]

    for r in range(t):
        _dispatch_row_copy(xext_ref.at[slot], xd_ref, sems.at[slot], r,
                           dest_ref[first + r]).start(priority=r % DMA_PRIORITIES)

    gate = jax.nn.sigmoid(jnp.dot(x1.astype(BF16), wpg_ref[...], preferred_element_type=F32))
    proj = jnp.dot(p_ref[...].astype(BF16), wpp_ref[...], preferred_element_type=F32)
    base_ref[...] = ALPHA * x1 + gate * proj

    def drain(s):
        for r in range(t):
            _dispatch_row_copy(xext_ref.at[s], xd_ref, sems.at[s], 0, 0).wait()

    @pl.when(step > 0)
    def _():
        drain(1 - slot)

    @pl.when(step == pl.num_programs(0) - 1)
    def _():
        drain(slot)


def _ple_dispatch(dest, zstart, n_valid, x1, p, layer, gwt, wpg, wpp, xd_rows):
    m, d = x1.shape
    t = DISPATCH_T
    return pl.pallas_call(
        _ple_dispatch_kernel,
        grid_spec=pltpu.PrefetchScalarGridSpec(
            num_scalar_prefetch=3,
            grid=(m // t,),
            in_specs=[
                pl.BlockSpec((t, d), lambda i, *_: (i, 0)),
                pl.BlockSpec((None, t, PLE_DIM), lambda i, *_: (layer, i, 0)),
                pl.BlockSpec((t, LANES), lambda i, *_: (i, 0)),
                _const_spec((d, d)),
                _layer_spec((PLE_DIM, d), layer),
            ],
            out_specs=[
                pl.BlockSpec((t, d), lambda i, *_: (i, 0)),
                pl.BlockSpec(memory_space=pl.ANY),
            ],
            scratch_shapes=[
                pltpu.VMEM((2, t, XD_WIDTH), F32),
                pltpu.VMEM((EXPERT_TM, XD_WIDTH), F32),
                pltpu.SemaphoreType.DMA((2,)),
                pltpu.SemaphoreType.DMA(()),
            ],
        ),
        out_shape=[jax.ShapeDtypeStruct((m, d), F32), jax.ShapeDtypeStruct((xd_rows, XD_WIDTH), F32)],
        compiler_params=pltpu.CompilerParams(
            dimension_semantics=("arbitrary",), vmem_limit_bytes=VMEM_LIMIT),
        name="ple_dispatch",
    )(dest, zstart, n_valid, x1, p, gwt, wpg, wpp)


def _ffn(xb, wg_ref, wu_ref, wd_ref):
    g = jnp.dot(xb, wg_ref[...], preferred_element_type=F32)
    u = jnp.dot(xb, wu_ref[...], preferred_element_type=F32)
    h = (jax.nn.silu(g) * u).astype(BF16)
    return jnp.dot(h, wd_ref[...], preferred_element_type=F32)


def _expert_kernel(lo_ref, hi_ref, rows_ref, nv_ref, x_ref, wgl_ref, wul_ref, wdl_ref,
                   wgh_ref, wuh_ref, wdh_ref, y_ref):
    del lo_ref, hi_ref
    b = pl.program_id(0)
    tm = x_ref.shape[0]
    half = tm // 2

    def run(rows):
        xb = x_ref[:rows, :D_MODEL].astype(BF16)
        w_lo = x_ref[:rows, D_MODEL:D_MODEL + 1]
        w_hi = x_ref[:rows, D_MODEL + 1:D_MODEL + 2]
        y_ref[:rows, :] = (w_lo * _ffn(xb, wgl_ref, wul_ref, wdl_ref)
                           + w_hi * _ffn(xb, wgh_ref, wuh_ref, wdh_ref))

    valid = b < nv_ref[0]
    token_rows = rows_ref[b]

    @pl.when(valid & (token_rows > half))
    def _():
        run(tm)

    @pl.when(valid & (token_rows <= half))
    def _():
        run(half)
        y_ref[half:, :] = jnp.zeros((tm - half, y_ref.shape[1]), y_ref.dtype)

    @pl.when(jnp.logical_not(valid))
    def _():
        y_ref[...] = jnp.zeros_like(y_ref)


def _experts(blk_lo, blk_hi, blk_rows, n_valid, xd, wg, wu, wd, n_blocks):
    d = D_MODEL
    tm = EXPERT_TM

    def x_map(b, lo, hi, rows, nv):
        return (jnp.minimum(b, jnp.maximum(nv[0] - 1, 0)), 0)

    def lo_map(b, lo, hi, rows, nv):
        return (lo[b], 0, 0)

    def hi_map(b, lo, hi, rows, nv):
        return (hi[b], 0, 0)

    return pl.pallas_call(
        _expert_kernel,
        grid_spec=pltpu.PrefetchScalarGridSpec(
            num_scalar_prefetch=4,
            grid=(n_blocks,),
            in_specs=[
                pl.BlockSpec((tm, XD_WIDTH), x_map),
                pl.BlockSpec((None, d, D_EXPERT), lo_map),
                pl.BlockSpec((None, d, D_EXPERT), lo_map),
                pl.BlockSpec((None, D_EXPERT, d), lo_map),
                pl.BlockSpec((None, d, D_EXPERT), hi_map),
                pl.BlockSpec((None, d, D_EXPERT), hi_map),
                pl.BlockSpec((None, D_EXPERT, d), hi_map),
            ],
            out_specs=pl.BlockSpec((tm, d), lambda b, lo, hi, rows, nv: (b, 0)),
        ),
        out_shape=jax.ShapeDtypeStruct((n_blocks * tm, d), F32),
        compiler_params=pltpu.CompilerParams(
            dimension_semantics=("arbitrary",), vmem_limit_bytes=EXPERT_VMEM_LIMIT),
        name="experts",
    )(blk_lo, blk_hi, blk_rows, n_valid, xd, wg, wu, wd, wg, wu, wd)


def _combine_row_copy(y_ref, buf_ref, sems, slot, r, src_row):
    return pltpu.make_async_copy(y_ref.at[pl.ds(src_row, 1), :], buf_ref.at[slot, pl.ds(r, 1), :],
                                 sems.at[slot])


def _final_kernel(dest_ref, base_ref, g_ref, b_ref, y_ref, xo_ref, xob_ref, buf_ref, sems):
    t = base_ref.shape[0]
    i = pl.program_id(0)
    last = pl.num_programs(0) - 1

    def issue(tile, slot):
        for r in range(t):
            _combine_row_copy(y_ref, buf_ref, sems, slot, r,
                              dest_ref[tile * t + r]).start(priority=r % DMA_PRIORITIES)

    def drain(slot):
        for r in range(t):
            _combine_row_copy(y_ref, buf_ref, sems, slot, 0, 0).wait()

    @pl.when(i == 0)
    def _():
        issue(0, 0)
        issue(jnp.minimum(1, last), 1)

    slot = i % FINAL_SLOTS
    drain(slot)
    xo_ref[...] = base_ref[...] + buf_ref[slot]
    issue(jnp.minimum(i + 2, last), (i + 2) % FINAL_SLOTS)
    x2 = _layer_norm(xo_ref[...], g_ref[...], b_ref[...])
    xo_ref[...] = x2
    xob_ref[...] = x2.astype(BF16)

    @pl.when(i == last)
    def _():
        drain((i + 1) % FINAL_SLOTS)
        drain((i + 2) % FINAL_SLOTS)


def _final(dest, base, g, b, y):
    m, d = base.shape
    t = FINAL_T
    return pl.pallas_call(
        _final_kernel,
        grid_spec=pltpu.PrefetchScalarGridSpec(
            num_scalar_prefetch=1,
            grid=(m // t,),
            in_specs=[
                pl.BlockSpec((t, d), lambda i, dest: (i, 0)),
                pl.BlockSpec((1, d), lambda i, dest: (0, 0), pipeline_mode=pl.Buffered(1)),
                pl.BlockSpec((1, d), lambda i, dest: (0, 0), pipeline_mode=pl.Buffered(1)),
                pl.BlockSpec(memory_space=pl.ANY),
            ],
            out_specs=[
                pl.BlockSpec((t, d), lambda i, dest: (i, 0)),
                pl.BlockSpec((t, d), lambda i, dest: (i, 0)),
            ],
            scratch_shapes=[pltpu.VMEM((FINAL_SLOTS, t, d), F32), pltpu.SemaphoreType.DMA((FINAL_SLOTS,))],
        ),
        out_shape=[jax.ShapeDtypeStruct((m, d), F32), jax.ShapeDtypeStruct((m, d), BF16)],
        compiler_params=pltpu.CompilerParams(
            dimension_semantics=("arbitrary",), vmem_limit_bytes=VMEM_LIMIT),
        name="final",
    )(dest, base, g, b, y)


def kernel(x, p, w_in, gmlp_ln_g, gmlp_ln_b, gmlp_ws, gmlp_bs, w_out_a, w_out_b, w_o, ln1_g, ln1_b,
           router_w, router_bias, exp_w_gate, exp_w_up, exp_w_down, ple_w_gate, ple_w_proj,
           ln2_g, ln2_b):
    batch, seq, d = x.shape
    m = batch * seq
    tm = EXPERT_TM
    n_blocks = m // tm + N_BUCKETS
    xd_rows = (n_blocks + 1) * tm

    xf = x.reshape(m, d)
    xb = xf.astype(BF16)
    p_flat = p.reshape(DEPTH, m, PLE_DIM)
    rw = jnp.pad(router_w.astype(F32), ((0, 0), (0, LANES - N_EXPERTS)))
    rwh = rw.astype(BF16)
    rw2 = jnp.concatenate([rwh, (rw - rwh.astype(F32)).astype(BF16)], axis=1)
    rb = router_bias.astype(F32).reshape(N_EXPERTS, 1)
    bucket_ids = jnp.arange(N_BUCKETS, dtype=jnp.int32)
    block_ids = jnp.arange(n_blocks, dtype=jnp.int32)
    wpp_b = ple_w_proj.astype(BF16)
    wg_rows = exp_w_gate.reshape(DEPTH, N_EXPERTS * D_MODEL, D_EXPERT)
    wu_rows = exp_w_up.reshape(DEPTH, N_EXPERTS * D_MODEL, D_EXPERT)
    wd_rows = exp_w_down.reshape(DEPTH, N_EXPERTS * D_EXPERT, D_MODEL)

    for i in range(DEPTH):
        ug, wd_b = _inproj(xb, w_in, i, 0, 2 * A_WIDTH, jax.nn.gelu, cast=(wd_rows,))
        qkv = _inproj(xb, w_in, i, 2 * A_WIDTH, 3 * B_WIDTH, lambda a: a,
                      first_tile_scale=B_HEAD_DIM ** -0.5 * LOG2E)
        gates, wg_b, wu_b, w_out_a_b, w_out_b_b, w_o_b, wpg_b = _inproj(
            xb, w_in, i, 2 * A_WIDTH + 3 * B_WIDTH, 2 * D_MODEL, jax.nn.sigmoid,
            cast=(wg_rows, wu_rows, w_out_a, w_out_b, w_o, ple_w_gate))
        wg_b = wg_b.reshape(N_EXPERTS, D_MODEL, D_EXPERT)
        wu_b = wu_b.reshape(N_EXPERTS, D_MODEL, D_EXPERT)
        wd_b = wd_b.reshape(N_EXPERTS, D_EXPERT, D_MODEL)

        bias = jnp.repeat(jnp.transpose(gmlp_bs[i]), A_HEAD_DIM, axis=1)
        ya = _gmlp(ug, gmlp_ln_g[i].reshape(1, A_WIDTH), gmlp_ln_b[i].reshape(1, A_WIDTH),
                   gmlp_ws, i, bias, w_out_a_b)
        ob = _attention(qkv, batch, seq)

        x1, br, gwt, cnt = _merge(
            ob, ya, gates, xf, w_out_b_b, w_o_b,
            ln1_g[i].reshape(1, d), ln1_b[i].reshape(1, d), rw2, rb)

        counts = cnt[:N_BUCKETS, 0].astype(jnp.int32)
        nblk = jnp.maximum((counts + tm - 1) // tm, 1)
        blk_end = jnp.cumsum(nblk)
        pstart = (blk_end - nblk) * tm
        dest = jnp.sum(jnp.where(br[0][None, :] == bucket_ids[:, None], pstart[:, None], 0), axis=0) + br[1]
        zstart = pstart + counts // SUBLANES * SUBLANES
        blk_bucket = jnp.minimum(
            jnp.sum((block_ids[:, None] >= blk_end[None, :]).astype(jnp.int32), axis=1), N_BUCKETS - 1)
        blk_lo = jnp.asarray(BUCKET_LO)[blk_bucket]
        blk_hi = jnp.asarray(BUCKET_HI)[blk_bucket]
        blk_rows = jnp.clip(counts[blk_bucket] - (block_ids - (blk_end - nblk)[blk_bucket]) * tm, 0, tm)
        n_valid = blk_end[-1:].astype(jnp.int32)

        base, xd = _ple_dispatch(dest, zstart, n_valid, x1, p_flat, i, gwt, wpg_b, wpp_b, xd_rows)
        y = _experts(blk_lo, blk_hi, blk_rows, n_valid, xd, wg_b, wu_b, wd_b, n_blocks)
        xf, xb = _final(dest, base, ln2_g[i].reshape(1, d), ln2_b[i].reshape(1, d), y)

    return xf.reshape(batch, seq, d)
```

```python
import functools

import jax
import jax.numpy as jnp
import numpy as np
from jax import lax
from jax.experimental import pallas as pl
from jax.experimental.pallas import tpu as pltpu

D_MODEL = 2048
DEPTH = 4
A_HEADS = 8
A_HEAD_DIM = 128
A_WIDTH = A_HEADS * A_HEAD_DIM
CHUNK = 128
B_HEADS = 8
B_HEAD_DIM = 128
B_WIDTH = B_HEADS * B_HEAD_DIM
N_EXPERTS = 16
N_GROUPS = 4
EXPERTS_PER_GROUP = N_EXPERTS // N_GROUPS
D_EXPERT = 1024
PLE_DIM = 256
ALPHA = (2 * DEPTH) ** 0.25
LN_EPS = 1e-5
LOG2E = 1.4426950408889634

LANES = 128
SUBLANES = 8
VMEM_LIMIT = 56 * 1024 * 1024
EXPERT_VMEM_LIMIT = 62 * 1024 * 1024

F32 = jnp.float32
BF16 = jnp.bfloat16

PAIRS = [(lo, hi) for lo in range(EXPERTS_PER_GROUP) for hi in range(lo + 1, EXPERTS_PER_GROUP)]
PAIRS_PER_GROUP = len(PAIRS)
N_BUCKETS = N_GROUPS * PAIRS_PER_GROUP
BUCKET_ROWS = 32
BUCKET_LO = np.array([g * EXPERTS_PER_GROUP + lo for g in range(N_GROUPS) for lo, _ in PAIRS], np.int32)
BUCKET_HI = np.array([g * EXPERTS_PER_GROUP + hi for g in range(N_GROUPS) for _, hi in PAIRS], np.int32)
PAIR_BASE = [PAIRS.index((lo, lo + 1)) for lo in range(EXPERTS_PER_GROUP - 1)]
XD_WIDTH = D_MODEL + LANES

INPROJ_TM = 1024
INPROJ_TN = 1024
INPROJ_CHUNKS = 4
GMLP_T = 512
GMLP_PARTS = 2
ATTN_T = 256
ATTN_HEADS_PER_STEP = 4
ATTN_ROW_PARTS = 1
MERGE_T = 256
DISPATCH_T = 256
EXPERT_TM = 256
FINAL_T = 256
FINAL_SLOTS = 3
DMA_PRIORITIES = 2


def _layer_norm(xf, gain, bias):
    mu = jnp.mean(xf, axis=-1, keepdims=True)
    xc = xf - mu
    var = jnp.mean(xc * xc, axis=-1, keepdims=True)
    return xc * lax.rsqrt(var + LN_EPS) * gain + bias


def _const_spec(shape):
    nd = len(shape)
    return pl.BlockSpec(shape, lambda *_: (0,) * nd, pipeline_mode=pl.Buffered(1))


def _layer_spec(shape, layer):
    nd = len(shape)
    return pl.BlockSpec((None,) + tuple(shape), lambda *_: (layer,) + (0,) * nd,
                        pipeline_mode=pl.Buffered(1))


def _inproj_kernel(x_ref, w_ref, *refs, act, first_tile_scale, n_cast):
    cast_src, o_ref, cast_dst, wb_ref = refs[:n_cast], refs[n_cast], refs[n_cast + 1:-1], refs[-1]

    @pl.when(pl.program_id(1) == 0)
    def _():
        wb_ref[...] = w_ref[...].astype(BF16)

    for src, dst in zip(cast_src, cast_dst):
        dst[...] = src[...].astype(BF16)

    rows = x_ref.shape[0] // INPROJ_CHUNKS
    for c in range(INPROJ_CHUNKS):
        rs = slice(c * rows, (c + 1) * rows)
        y = act(jnp.dot(x_ref[rs, :], wb_ref[...], preferred_element_type=F32))
        if first_tile_scale is not None:
            y = y * jnp.where(pl.program_id(0) == 0, first_tile_scale, 1.0)
        o_ref[rs, :] = y.astype(o_ref.dtype)


def _inproj(xb, w, layer, col0, ncols, act, first_tile_scale=None, cast=()):
    m, k = xb.shape
    tm, tn = INPROJ_TM, INPROJ_TN
    cb0 = col0 // tn
    n_i = m // tm
    steps = (ncols // tn) * n_i
    cast_in_specs, cast_out_specs, cast_shapes = [], [], []
    for arr in cast:
        _, rows, cols = arr.shape
        slab = rows // steps
        cast_in_specs.append(pl.BlockSpec((None, slab, cols), lambda j, i: (layer, j * n_i + i, 0)))
        cast_out_specs.append(pl.BlockSpec((slab, cols), lambda j, i: (j * n_i + i, 0)))
        cast_shapes.append(jax.ShapeDtypeStruct((rows, cols), BF16))
    out = pl.pallas_call(
        functools.partial(_inproj_kernel, act=act, first_tile_scale=first_tile_scale, n_cast=len(cast)),
        grid=(ncols // tn, n_i),
        in_specs=[
            pl.BlockSpec((tm, k), lambda j, i: (i, 0)),
            pl.BlockSpec((None, k, tn), lambda j, i: (layer, 0, cb0 + j), pipeline_mode=pl.Buffered(1)),
        ] + cast_in_specs,
        out_specs=[pl.BlockSpec((tm, tn), lambda j, i: (i, j))] + cast_out_specs,
        out_shape=[jax.ShapeDtypeStruct((m, ncols), BF16)] + cast_shapes,
        scratch_shapes=[pltpu.VMEM((k, tn), BF16)],
        compiler_params=pltpu.CompilerParams(
            dimension_semantics=("arbitrary", "arbitrary"), vmem_limit_bytes=VMEM_LIMIT),
        name="inproj",
    )(xb, w, *cast)
    return out if cast else out[0]


def _gmlp_kernel(u_ref, gv_ref, lng_ref, lnb_ref, ws_ref, bias_ref, wo_ref, o_ref, yin_ref):
    t = u_ref.shape[0]
    row = lax.broadcasted_iota(jnp.int32, (CHUNK, CHUNK), 0)
    col = lax.broadcasted_iota(jnp.int32, (CHUNK, CHUNK), 1)
    tril = row >= col
    wsm = [jnp.where(tril, ws_ref[h], 0.0).astype(BF16) for h in range(A_HEADS)]
    tp = t // GMLP_PARTS
    nc = tp // CHUNK
    for part in range(GMLP_PARTS):
        r0 = part * tp
        v = _layer_norm(gv_ref[r0:r0 + tp, :].astype(F32), lng_ref[...], lnb_ref[...]).astype(BF16)
        for h in range(A_HEADS):
            hs = slice(h * A_HEAD_DIM, (h + 1) * A_HEAD_DIM)
            vh = jnp.concatenate([v[c * CHUNK:(c + 1) * CHUNK, hs] for c in range(nc)], axis=1)
            f = jnp.dot(wsm[h], vh, preferred_element_type=F32)
            bh = bias_ref[:, hs]
            for c in range(nc):
                rs = slice(r0 + c * CHUNK, r0 + (c + 1) * CHUNK)
                fc = f[:, c * A_HEAD_DIM:(c + 1) * A_HEAD_DIM] + bh
                yin_ref[rs, hs] = (u_ref[rs, hs].astype(F32) * fc).astype(BF16)
        o_ref[r0:r0 + tp, :] = jnp.dot(yin_ref[r0:r0 + tp, :], wo_ref[...],
                                       preferred_element_type=F32).astype(o_ref.dtype)


def _gmlp(ug, lng, lnb, ws, layer, bias, wo):
    m = ug.shape[0]
    t = GMLP_T
    return pl.pallas_call(
        _gmlp_kernel,
        grid=(m // t,),
        in_specs=[
            pl.BlockSpec((t, A_WIDTH), lambda i: (i, 0)),
            pl.BlockSpec((t, A_WIDTH), lambda i: (i, 1)),
            _const_spec((1, A_WIDTH)),
            _const_spec((1, A_WIDTH)),
            _layer_spec((A_HEADS, CHUNK, CHUNK), layer),
            _const_spec((CHUNK, A_WIDTH)),
            _const_spec((A_WIDTH, D_MODEL)),
        ],
        out_specs=pl.BlockSpec((t, D_MODEL), lambda i: (i, 0)),
        out_shape=jax.ShapeDtypeStruct((m, D_MODEL), BF16),
        scratch_shapes=[pltpu.VMEM((t, A_WIDTH), BF16)],
        compiler_params=pltpu.CompilerParams(
            dimension_semantics=("parallel",), vmem_limit_bytes=VMEM_LIMIT),
        name="gmlp",
    )(ug, ug, lng, lnb, ws, bias, wo)


def _attn_kernel(q_ref, k_ref, v_ref, o_ref):
    t = q_ref.shape[0]
    nh = q_ref.shape[1] // B_HEAD_DIM
    heads = range(nh)
    i = pl.program_id(2)
    row = lax.broadcasted_iota(jnp.int32, (t, t), 0)
    col = lax.broadcasted_iota(jnp.int32, (t, t), 1)
    neg_upper = jnp.where(row > col, -1.0, 0.0).astype(BF16)
    causal = col < row
    hs = [slice(h * B_HEAD_DIM, (h + 1) * B_HEAD_DIM) for h in heads]
    rp = t // ATTN_ROW_PARTS
    units = [(h, slice(r * rp, (r + 1) * rp)) for h in heads for r in range(ATTN_ROW_PARTS)]
    qs = [q_ref[rs, hs[h]] for h, rs in units]
    masks = [causal[rs, :] for _, rs in units]

    def tiles(off, state, diag):
        z = [lax.dot_general(qs[u], k_ref[pl.ds(off, t), hs[h]], (((1,), (1,)), ((), ())),
                             preferred_element_type=F32) for u, (h, _) in enumerate(units)]
        sp, logsig, packed = [], [], []
        for u in range(len(units)):
            neg_abs = pltpu.bitcast(pltpu.bitcast(z[u], jnp.uint32) | jnp.uint32(0x80000000), F32)
            s = jnp.maximum(z[u], 0.0) + jnp.log(1.0 + jnp.exp2(neg_abs)) * LOG2E
            logsig.append(z[u] - s)
            if diag:
                s = jnp.where(masks[u], s, 0.0)
            sp.append(s)
            packed.append(s.astype(BF16))
        suffix = [jnp.dot(packed[u], neg_upper, preferred_element_type=F32) for u in range(len(units))]
        a = []
        for u in range(len(units)):
            au = jnp.exp2(logsig[u] + suffix[u] + state[u][0])
            if diag:
                au = jnp.where(masks[u], au, 0.0)
            a.append(au.astype(BF16))
        return tuple(
            (state[u][0] - jnp.sum(sp[u], axis=1, keepdims=True),
             state[u][1] + jnp.dot(a[u], v_ref[pl.ds(off, t), hs[h]], preferred_element_type=F32))
            for u, (h, _) in enumerate(units))

    init = tuple((jnp.zeros((rp, 1), F32), jnp.zeros((rp, B_HEAD_DIM), F32)) for _ in units)
    state = tiles(pl.multiple_of(i * t, t), init, True)

    def body(jj, st):
        st = tiles(pl.multiple_of((i - 1 - 2 * jj) * t, t), st, False)
        return tiles(pl.multiple_of((i - 2 - 2 * jj) * t, t), st, False)

    state = lax.fori_loop(0, i // 2, body, state)
    state = lax.fori_loop(0, i % 2, lambda jj, st: tiles(0, st, False), state)
    for u, (h, rs) in enumerate(units):
        o_ref[rs, hs[h]] = state[u][1].astype(o_ref.dtype)


def _attention(qkv, batch, seq):
    m = qkv.shape[0]
    t = ATTN_T
    nq = seq // t
    w = ATTN_HEADS_PER_STEP * B_HEAD_DIM
    hb = B_WIDTH // w
    return pl.pallas_call(
        _attn_kernel,
        grid=(batch, hb, nq),
        in_specs=[
            pl.BlockSpec((t, w), lambda b, h, i: (b * nq + i, h)),
            pl.BlockSpec((seq, w), lambda b, h, i: (b, hb + h)),
            pl.BlockSpec((seq, w), lambda b, h, i: (b, 2 * hb + h)),
        ],
        out_specs=pl.BlockSpec((t, w), lambda b, h, i: (b * nq + i, h)),
        out_shape=jax.ShapeDtypeStruct((m, B_WIDTH), BF16),
        compiler_params=pltpu.CompilerParams(
            dimension_semantics=("parallel", "parallel", "parallel"), vmem_limit_bytes=VMEM_LIMIT),
        name="attn",
    )(qkv, qkv, qkv)


def _first_max(vals):
    best, idx = vals[0], jnp.zeros(vals[0].shape, jnp.int32)
    for j in range(1, len(vals)):
        better = vals[j] > best
        best = jnp.where(better, vals[j], best)
        idx = jnp.where(better, j, idx)
    return best, idx


def _pick(idx, vals):
    out = vals[0]
    for j in range(1, len(vals)):
        out = jnp.where(idx == j, vals[j], out)
    return out


def _merge_kernel(ob_ref, ya_ref, sga_ref, sgb_ref, x_ref, wob_ref, wo_ref, g_ref, b_ref,
                  rw2_ref, rb_ref, x1_ref, br_ref, gwt_ref, cnt_ref, base_ref, logit_ref):
    t = x_ref.shape[0]
    step = pl.program_id(0)

    @pl.when(step == 0)
    def _():
        base_ref[...] = jnp.zeros_like(base_ref)
        logit_ref[...] = jnp.zeros_like(logit_ref)

    live = jnp.where(step > 0, 1.0, 0.0)

    lt = jnp.transpose(logit_ref[...])[:N_EXPERTS, :]
    aff = jax.nn.sigmoid(lt)
    sel = aff + rb_ref[...]
    sel_rows = [sel[e:e + 1, :] for e in range(N_EXPERTS)]
    aff_rows = [aff[e:e + 1, :] for e in range(N_EXPERTS)]
    gscore = []
    for g in range(N_GROUPS):
        s0, s1, s2, s3 = sel_rows[4 * g:4 * g + 4]
        hi1, lo1 = jnp.maximum(s0, s1), jnp.minimum(s0, s1)
        hi2, lo2 = jnp.maximum(s2, s3), jnp.minimum(s2, s3)
        top1 = jnp.maximum(hi1, hi2)
        top2 = jnp.maximum(jnp.minimum(hi1, hi2), jnp.maximum(lo1, lo2))
        gscore.append(top1 + top2)
    _, gidx = _first_max(gscore)
    within = [_pick(gidx, [sel_rows[4 * g + j] for g in range(N_GROUPS)])
              for j in range(EXPERTS_PER_GROUP)]
    awithin = [_pick(gidx, [aff_rows[4 * g + j] for g in range(N_GROUPS)])
               for j in range(EXPERTS_PER_GROUP)]
    _, i0 = _first_max(within)
    masked = [jnp.where(i0 == j, -jnp.inf, within[j]) for j in range(EXPERTS_PER_GROUP)]
    _, i1 = _first_max(masked)
    a0, a1 = _pick(i0, awithin), _pick(i1, awithin)
    denom = a0 + a1
    w0, w1 = a0 / denom, a1 / denom

    swap = i1 < i0
    lo = jnp.where(swap, i1, i0)
    hi = jnp.where(swap, i0, i1)
    w_lo = jnp.where(swap, w1, w0)
    w_hi = jnp.where(swap, w0, w1)
    pair = _pick(lo, PAIR_BASE) + (hi - lo - 1)
    bucket = gidx * PAIRS_PER_GROUP + pair

    biota = lax.broadcasted_iota(jnp.int32, (BUCKET_ROWS, t), 0)
    oh = jnp.where(biota == bucket, 1.0, 0.0)
    row = lax.broadcasted_iota(jnp.int32, (t, t), 0)
    col = lax.broadcasted_iota(jnp.int32, (t, t), 1)
    before = jnp.where(row < col, 1.0, 0.0).astype(BF16)
    tot = base_ref[:, 0:1] + jnp.dot(oh.astype(BF16), before, preferred_element_type=F32)
    rank = jnp.sum(oh * tot, axis=0, keepdims=True)
    slot = lax.broadcasted_iota(jnp.int32, (2, t), 0)
    br_ref[...] = jnp.where(slot == 0, bucket, rank.astype(jnp.int32))
    newbase = base_ref[...] + live * jnp.sum(oh, axis=1, keepdims=True)
    base_ref[...] = newbase
    cnt_ref[...] = newbase

    wrow = lax.broadcasted_iota(jnp.int32, (LANES, t), 0)
    gw = jnp.where(wrow == 0, w_lo, jnp.where(wrow == 1, w_hi, 0.0))
    gwt_ref[...] = jnp.transpose(gw)

    yb = jnp.dot(ob_ref[...], wob_ref[...], preferred_element_type=F32)
    mixin = sga_ref[...].astype(F32) * ya_ref[...].astype(F32) + sgb_ref[...].astype(F32) * yb
    mix = jnp.dot(mixin.astype(BF16), wo_ref[...], preferred_element_type=F32)
    x1 = _layer_norm(ALPHA * x_ref[...] + mix, g_ref[...], b_ref[...])
    x1_ref[...] = x1
    x_hi = x1.astype(BF16)
    x_lo = (x1 - x_hi.astype(F32)).astype(BF16)
    r_hi = jnp.dot(x_hi, rw2_ref[...], preferred_element_type=F32)
    r_lo = jnp.dot(x_lo, rw2_ref[...], preferred_element_type=F32)
    logit_ref[...] = ((r_hi[:, :LANES] + r_hi[:, LANES:])
                      + (r_lo[:, :LANES] + r_lo[:, LANES:]))


def _merge(ob, ya, gates, xf, wob, wo, g, b, rw2, rb):
    m = xf.shape[0]
    t = MERGE_T
    last = m // t - 1

    def cur(col):
        return lambda i: (jnp.minimum(i, last), col)

    return pl.pallas_call(
        _merge_kernel,
        grid=(m // t + 1,),
        in_specs=[
            pl.BlockSpec((t, B_WIDTH), cur(0)),
            pl.BlockSpec((t, D_MODEL), cur(0)),
            pl.BlockSpec((t, D_MODEL), cur(0)),
            pl.BlockSpec((t, D_MODEL), cur(1)),
            pl.BlockSpec((t, D_MODEL), cur(0)),
            _const_spec((B_WIDTH, D_MODEL)),
            _const_spec((D_MODEL, D_MODEL)),
            _const_spec((1, D_MODEL)),
            _const_spec((1, D_MODEL)),
            _const_spec((D_MODEL, 2 * LANES)),
            _const_spec((N_EXPERTS, 1)),
        ],
        out_specs=[
            pl.BlockSpec((t, D_MODEL), cur(0)),
            pl.BlockSpec((2, t), lambda i: (0, jnp.maximum(i - 1, 0))),
            pl.BlockSpec((t, LANES), lambda i: (jnp.maximum(i - 1, 0), 0)),
            pl.BlockSpec((BUCKET_ROWS, LANES), lambda i: (0, 0)),
        ],
        out_shape=[
            jax.ShapeDtypeStruct((m, D_MODEL), F32),
            jax.ShapeDtypeStruct((2, m), jnp.int32),
            jax.ShapeDtypeStruct((m, LANES), F32),
            jax.ShapeDtypeStruct((BUCKET_ROWS, LANES), F32),
        ],
        scratch_shapes=[pltpu.VMEM((BUCKET_ROWS, LANES), F32), pltpu.VMEM((t, LANES), F32)],
        compiler_params=pltpu.CompilerParams(
            dimension_semantics=("arbitrary",), vmem_limit_bytes=VMEM_LIMIT),
        name="merge",
    )(ob, ya, gates, gates, xf, wob, wo, g, b, rw2, rb)


def _dispatch_row_copy(xext_ref, xd_ref, sem, src_row, dst_row):
    return pltpu.make_async_copy(xext_ref.at[pl.ds(src_row, 1), :], xd_ref.at[pl.ds(dst_row, 1), :], sem)


def _zero_fill_copy(zero_ref, xd_ref, sem, dst_row):
    rows = zero_ref.shape[0]
    return pltpu.make_async_copy(zero_ref, xd_ref.at[pl.ds(pl.multiple_of(dst_row, SUBLANES), rows), :], sem)


def _ple_dispatch_kernel(dest_ref, zstart_ref, nv_ref, x1_ref, p_ref, gwt_ref, wpg_ref, wpp_ref,
                         base_ref, xd_ref, xext_ref, zero_ref, sems, zsem):
    t = x1_ref.shape[0]
    first = pl.program_id(0) * t

    @pl.when(pl.program_id(0) == 0)
    def _():
        zero_ref[...] = jnp.zeros_like(zero_ref)
        for k in range(N_BUCKETS):
            _zero_fill_copy(zero_ref, xd_ref, zsem, zstart_ref[k]).start()
        for k in range(N_BUCKETS):
            _zero_fill_copy(zero_ref, xd_ref, zsem, zstart_ref[k]).wait()
        rows = zero_ref.shape[0]
        total_blocks = xd_ref.shape[0] // rows

        def fill(b, c):
            _zero_fill_copy(zero_ref, xd_ref, zsem, b * rows).start()
            return c

        def fill_done(b, c):
            _zero_fill_copy(zero_ref, xd_ref, zsem, b * rows).wait()
            return c

        lax.fori_loop(nv_ref[0], total_blocks, fill, 0)
        lax.fori_loop(nv_ref[0], total_blocks, fill_done, 0)

    step = pl.program_id(0)
    slot = step % 2
    x1 = x1_ref[...]
    xext_ref[slot, :, :D_MODEL] = x1
    xext_ref[slot, :, D_MODEL:] = gwt_ref[...]

    for r in range(t):
        _dispatch_row_copy(xext_ref.at[slot], xd_ref, sems.at[slot], r,
                           dest_ref[first + r]).start(priority=r % DMA_PRIORITIES)

    gate = jax.nn.sigmoid(jnp.dot(x1.astype(BF16), wpg_ref[...], preferred_element_type=F32))
    proj = jnp.dot(p_ref[...].astype(BF16), wpp_ref[...], preferred_element_type=F32)
    base_ref[...] = ALPHA * x1 + gate * proj

    def drain(s):
        for r in range(t):
            _dispatch_row_copy(xext_ref.at[s], xd_ref, sems.at[s], 0, 0).wait()

    @pl.when(step > 0)
    def _():
        drain(1 - slot)

    @pl.when(step == pl.num_programs(0) - 1)
    def _():
        drain(slot)


def _ple_dispatch(dest, zstart, n_valid, x1, p, layer, gwt, wpg, wpp, xd_rows):
    m, d = x1.shape
    t = DISPATCH_T
    return pl.pallas_call(
        _ple_dispatch_kernel,
        grid_spec=pltpu.PrefetchScalarGridSpec(
            num_scalar_prefetch=3,
            grid=(m // t,),
            in_specs=[
                pl.BlockSpec((t, d), lambda i, *_: (i, 0)),
                pl.BlockSpec((None, t, PLE_DIM), lambda i, *_: (layer, i, 0)),
                pl.BlockSpec((t, LANES), lambda i, *_: (i, 0)),
                _const_spec((d, d)),
                _layer_spec((PLE_DIM, d), layer),
            ],
            out_specs=[
                pl.BlockSpec((t, d), lambda i, *_: (i, 0)),
                pl.BlockSpec(memory_space=pl.ANY),
            ],
            scratch_shapes=[
                pltpu.VMEM((2, t, XD_WIDTH), F32),
                pltpu.VMEM((EXPERT_TM, XD_WIDTH), F32),
                pltpu.SemaphoreType.DMA((2,)),
                pltpu.SemaphoreType.DMA(()),
            ],
        ),
        out_shape=[jax.ShapeDtypeStruct((m, d), F32), jax.ShapeDtypeStruct((xd_rows, XD_WIDTH), F32)],
        compiler_params=pltpu.CompilerParams(
            dimension_semantics=("arbitrary",), vmem_limit_bytes=VMEM_LIMIT),
        name="ple_dispatch",
    )(dest, zstart, n_valid, x1, p, gwt, wpg, wpp)


def _ffn(xb, wg_ref, wu_ref, wd_ref):
    g = jnp.dot(xb, wg_ref[...], preferred_element_type=F32)
    u = jnp.dot(xb, wu_ref[...], preferred_element_type=F32)
    h = (jax.nn.silu(g) * u).astype(BF16)
    return jnp.dot(h, wd_ref[...], preferred_element_type=F32)


def _expert_kernel(lo_ref, hi_ref, rows_ref, nv_ref, x_ref, wgl_ref, wul_ref, wdl_ref,
                   wgh_ref, wuh_ref, wdh_ref, y_ref):
    del lo_ref, hi_ref
    b = pl.program_id(0)
    tm = x_ref.shape[0]
    half = tm // 2

    def run(rows):
        xb = x_ref[:rows, :D_MODEL].astype(BF16)
        w_lo = x_ref[:rows, D_MODEL:D_MODEL + 1]
        w_hi = x_ref[:rows, D_MODEL + 1:D_MODEL + 2]
        y_ref[:rows, :] = (w_lo * _ffn(xb, wgl_ref, wul_ref, wdl_ref)
                           + w_hi * _ffn(xb, wgh_ref, wuh_ref, wdh_ref))

    valid = b < nv_ref[0]
    token_rows = rows_ref[b]

    @pl.when(valid & (token_rows > half))
    def _():
        run(tm)

    @pl.when(valid & (token_rows <= half))
    def _():
        run(half)
        y_ref[half:, :] = jnp.zeros((tm - half, y_ref.shape[1]), y_ref.dtype)

    @pl.when(jnp.logical_not(valid))
    def _():
        y_ref[...] = jnp.zeros_like(y_ref)


def _experts(blk_lo, blk_hi, blk_rows, n_valid, xd, wg, wu, wd, n_blocks):
    d = D_MODEL
    tm = EXPERT_TM

    def x_map(b, lo, hi, rows, nv):
        return (jnp.minimum(b, jnp.maximum(nv[0] - 1, 0)), 0)

    def lo_map(b, lo, hi, rows, nv):
        return (lo[b], 0, 0)

    def hi_map(b, lo, hi, rows, nv):
        return (hi[b], 0, 0)

    return pl.pallas_call(
        _expert_kernel,
        grid_spec=pltpu.PrefetchScalarGridSpec(
            num_scalar_prefetch=4,
            grid=(n_blocks,),
            in_specs=[
                pl.BlockSpec((tm, XD_WIDTH), x_map),
                pl.BlockSpec((None, d, D_EXPERT), lo_map),
                pl.BlockSpec((None, d, D_EXPERT), lo_map),
                pl.BlockSpec((None, D_EXPERT, d), lo_map),
                pl.BlockSpec((None, d, D_EXPERT), hi_map),
                pl.BlockSpec((None, d, D_EXPERT), hi_map),
                pl.BlockSpec((None, D_EXPERT, d), hi_map),
            ],
            out_specs=pl.BlockSpec((tm, d), lambda b, lo, hi, rows, nv: (b, 0)),
        ),
        out_shape=jax.ShapeDtypeStruct((n_blocks * tm, d), F32),
        compiler_params=pltpu.CompilerParams(
            dimension_semantics=("arbitrary",), vmem_limit_bytes=EXPERT_VMEM_LIMIT),
        name="experts",
    )(blk_lo, blk_hi, blk_rows, n_valid, xd, wg, wu, wd, wg, wu, wd)


def _combine_row_copy(y_ref, buf_ref, sems, slot, r, src_row):
    return pltpu.make_async_copy(y_ref.at[pl.ds(src_row, 1), :], buf_ref.at[slot, pl.ds(r, 1), :],
                                 sems.at[slot])


def _final_kernel(dest_ref, base_ref, g_ref, b_ref, y_ref, xo_ref, xob_ref, buf_ref, sems):
    t = base_ref.shape[0]
    i = pl.program_id(0)
    last = pl.num_programs(0) - 1

    def issue(tile, slot):
        for r in range(t):
            _combine_row_copy(y_ref, buf_ref, sems, slot, r,
                              dest_ref[tile * t + r]).start(priority=r % DMA_PRIORITIES)

    def drain(slot):
        for r in range(t):
            _combine_row_copy(y_ref, buf_ref, sems, slot, 0, 0).wait()

    @pl.when(i == 0)
    def _():
        issue(0, 0)
        issue(jnp.minimum(1, last), 1)

    slot = i % FINAL_SLOTS
    drain(slot)
    xo_ref[...] = base_ref[...] + buf_ref[slot]
    issue(jnp.minimum(i + 2, last), (i + 2) % FINAL_SLOTS)
    x2 = _layer_norm(xo_ref[...], g_ref[...], b_ref[...])
    xo_ref[...] = x2
    xob_ref[...] = x2.astype(BF16)

    @pl.when(i == last)
    def _():
        drain((i + 1) % FINAL_SLOTS)
        drain((i + 2) % FINAL_SLOTS)


def _final(dest, base, g, b, y):
    m, d = base.shape
    t = FINAL_T
    return pl.pallas_call(
        _final_kernel,
        grid_spec=pltpu.PrefetchScalarGridSpec(
            num_scalar_prefetch=1,
            grid=(m // t,),
            in_specs=[
                pl.BlockSpec((t, d), lambda i, dest: (i, 0)),
                pl.BlockSpec((1, d), lambda i, dest: (0, 0), pipeline_mode=pl.Buffered(1)),
                pl.BlockSpec((1, d), lambda i, dest: (0, 0), pipeline_mode=pl.Buffered(1)),
                pl.BlockSpec(memory_space=pl.ANY),
            ],
            out_specs=[
                pl.BlockSpec((t, d), lambda i, dest: (i, 0)),
                pl.BlockSpec((t, d), lambda i, dest: (i, 0)),
            ],
            scratch_shapes=[pltpu.VMEM((FINAL_SLOTS, t, d), F32), pltpu.SemaphoreType.DMA((FINAL_SLOTS,))],
        ),
        out_shape=[jax.ShapeDtypeStruct((m, d), F32), jax.ShapeDtypeStruct((m, d), BF16)],
        compiler_params=pltpu.CompilerParams(
            dimension_semantics=("arbitrary",), vmem_limit_bytes=VMEM_LIMIT),
        name="final",
    )(dest, base, g, b, y)


def kernel(x, p, w_in, gmlp_ln_g, gmlp_ln_b, gmlp_ws, gmlp_bs, w_out_a, w_out_b, w_o, ln1_g, ln1_b,
           router_w, router_bias, exp_w_gate, exp_w_up, exp_w_down, ple_w_gate, ple_w_proj,
           ln2_g, ln2_b):
    batch, seq, d = x.shape
    m = batch * seq
    tm = EXPERT_TM
    n_blocks = m // tm + N_BUCKETS
    xd_rows = (n_blocks + 1) * tm

    xf = x.reshape(m, d)
    xb = xf.astype(BF16)
    p_flat = p.reshape(DEPTH, m, PLE_DIM)
    rw = jnp.pad(router_w.astype(F32), ((0, 0), (0, LANES - N_EXPERTS)))
    rwh = rw.astype(BF16)
    rw2 = jnp.concatenate([rwh, (rw - rwh.astype(F32)).astype(BF16)], axis=1)
    rb = router_bias.astype(F32).reshape(N_EXPERTS, 1)
    bucket_ids = jnp.arange(N_BUCKETS, dtype=jnp.int32)
    block_ids = jnp.arange(n_blocks, dtype=jnp.int32)
    wpp_b = ple_w_proj.astype(BF16)
    wg_rows = exp_w_gate.reshape(DEPTH, N_EXPERTS * D_MODEL, D_EXPERT)
    wu_rows = exp_w_up.reshape(DEPTH, N_EXPERTS * D_MODEL, D_EXPERT)
    wd_rows = exp_w_down.reshape(DEPTH, N_EXPERTS * D_EXPERT, D_MODEL)

    for i in range(DEPTH):
        ug, wd_b = _inproj(xb, w_in, i, 0, 2 * A_WIDTH, jax.nn.gelu, cast=(wd_rows,))
        qkv = _inproj(xb, w_in, i, 2 * A_WIDTH, 3 * B_WIDTH, lambda a: a,
                      first_tile_scale=B_HEAD_DIM ** -0.5 * LOG2E)
        gates, wg_b, wu_b, w_out_a_b, w_out_b_b, w_o_b, wpg_b = _inproj(
            xb, w_in, i, 2 * A_WIDTH + 3 * B_WIDTH, 2 * D_MODEL, jax.nn.sigmoid,
            cast=(wg_rows, wu_rows, w_out_a, w_out_b, w_o, ple_w_gate))
        wg_b = wg_b.reshape(N_EXPERTS, D_MODEL, D_EXPERT)
        wu_b = wu_b.reshape(N_EXPERTS, D_MODEL, D_EXPERT)
        wd_b = wd_b.reshape(N_EXPERTS, D_EXPERT, D_MODEL)

        bias = jnp.repeat(jnp.transpose(gmlp_bs[i]), A_HEAD_DIM, axis=1)
        ya = _gmlp(ug, gmlp_ln_g[i].reshape(1, A_WIDTH), gmlp_ln_b[i].reshape(1, A_WIDTH),
                   gmlp_ws, i, bias, w_out_a_b)
        ob = _attention(qkv, batch, seq)

        x1, br, gwt, cnt = _merge(
            ob, ya, gates, xf, w_out_b_b, w_o_b,
            ln1_g[i].reshape(1, d), ln1_b[i].reshape(1, d), rw2, rb)

        counts = cnt[:N_BUCKETS, 0].astype(jnp.int32)
        nblk = jnp.maximum((counts + tm - 1) // tm, 1)
        blk_end = jnp.cumsum(nblk)
        pstart = (blk_end - nblk) * tm
        dest = jnp.sum(jnp.where(br[0][None, :] == bucket_ids[:, None], pstart[:, None], 0), axis=0) + br[1]
        zstart = pstart + counts // SUBLANES * SUBLANES
        blk_bucket = jnp.minimum(
            jnp.sum((block_ids[:, None] >= blk_end[None, :]).astype(jnp.int32), axis=1), N_BUCKETS - 1)
        blk_lo = jnp.asarray(BUCKET_LO)[blk_bucket]
        blk_hi = jnp.asarray(BUCKET_HI)[blk_bucket]
        blk_rows = jnp.clip(counts[blk_bucket] - (block_ids - (blk_end - nblk)[blk_bucket]) * tm, 0, tm)
        n_valid = blk_end[-1:].astype(jnp.int32)

        base, xd = _ple_dispatch(dest, zstart, n_valid, x1, p_flat, i, gwt, wpg_b, wpp_b, xd_rows)
        y = _experts(blk_lo, blk_hi, blk_rows, n_valid, xd, wg_b, wu_b, wd_b, n_blocks)
        xf, xb = _final(dest, base, ln2_g[i].reshape(1, d), ln2_b[i].reshape(1, d), y)

    return xf.reshape(batch, seq, d)
```
